```python
import jax, jax.numpy as jnp
from jax import lax
import numpy as np

D_MODEL = 1024
BATCH = 2
SEQ = 16384
DEPTH = 1

HEAD_DIM = 128
DIL_PAIRS = ((128, 1), (512, 4), (2048, 16))
A_HEADS_PER_GROUP = 2
A_HEADS = A_HEADS_PER_GROUP * len(DIL_PAIRS)
B_Q_HEADS = 4
B_KV_HEADS = 2
B_WINDOW = 128
M_HEADS = 4
N_MEM = 256
D_FF = 2816
ROPE_THETA = 10000.0
BLOCK = 128
EPS = 1e-6
NEG_INF = -1e30

A_WIDTH = A_HEADS * HEAD_DIM
A_OUT = A_HEADS_PER_GROUP * HEAD_DIM
B_WIDTH = B_Q_HEADS * HEAD_DIM
B_KV_WIDTH = B_KV_HEADS * HEAD_DIM
M_WIDTH = M_HEADS * HEAD_DIM
D_IN = 3 * A_WIDTH + B_WIDTH + 2 * B_KV_WIDTH + M_WIDTH
IN_SPLITS = tuple(np.cumsum([A_WIDTH, A_WIDTH, A_WIDTH, B_WIDTH, B_KV_WIDTH, B_KV_WIDTH]).tolist())

kernel_name = "hybrid_dilated_swa_sink_memory_macaron"


def rms_norm(x, g):
    xf = x.astype(jnp.float32)
    y = xf * lax.rsqrt(jnp.mean(xf * xf, axis=-1, keepdims=True) + EPS)
    return (y * g.astype(jnp.float32)).astype(x.dtype)


def swiglu(x, w_in, w_out):
    gate, up = jnp.split(x @ w_in, 2, axis=-1)
    return (jax.nn.silu(gate) * up) @ w_out


def rope(x, pos):
    half = HEAD_DIM // 2
    inv = ROPE_THETA ** (-jnp.arange(half, dtype=jnp.float32) / half)
    ang = pos.astype(jnp.float32)[:, None] * inv[None, :]
    cos = jnp.cos(ang)[None, :, None, :]
    sin = jnp.sin(ang)[None, :, None, :]
    x1 = x[..., :half].astype(jnp.float32)
    x2 = x[..., half:].astype(jnp.float32)
    return jnp.concatenate([x1 * cos - x2 * sin, x2 * cos + x1 * sin], axis=-1).astype(x.dtype)


def banded_attention(q, k, v, max_dist, sink=None):
    b, L, hq, d = q.shape
    hkv = k.shape[2]
    grp = hq // hkv
    blk = min(BLOCK, L)
    nb = -(-L // blk)
    pad = nb * blk - L
    if pad:
        cfg = ((0, 0), (0, pad), (0, 0), (0, 0))
        q, k, v = jnp.pad(q, cfg), jnp.pad(k, cfg), jnp.pad(v, cfg)
    qb = q.reshape(b, nb, blk, hkv, grp, d)
    kb = k.reshape(b, nb, blk, hkv, d)
    vb = v.reshape(b, nb, blk, hkv, d)
    shift = ((0, 0), (1, 0), (0, 0), (0, 0), (0, 0))
    kk = jnp.concatenate([jnp.pad(kb, shift)[:, :-1], kb], axis=2)
    vv = jnp.concatenate([jnp.pad(vb, shift)[:, :-1], vb], axis=2)
    s = jnp.einsum("bnqhgd,bnkhd->bnhgqk", qb, kk).astype(jnp.float32) * (d ** -0.5)
    qpos = jnp.arange(blk)[:, None] + blk
    kpos = jnp.arange(2 * blk)[None, :]
    dist = qpos - kpos
    band = (dist >= 0) & (dist <= max_dist)
    has_prev = (jnp.arange(nb) > 0)[:, None, None] | (kpos >= blk)[None]
    mask = band[None] & has_prev
    s = jnp.where(mask[None, :, None, None], s, NEG_INF)
    m = jnp.max(s, axis=-1)
    if sink is not None:
        sk = sink.astype(jnp.float32).reshape(1, 1, hkv, grp, 1)
        m = jnp.maximum(m, sk)
    p = jnp.exp(s - m[..., None])
    den = jnp.sum(p, axis=-1)
    tot = den + jnp.exp(sk - m) if sink is not None else den
    o = jnp.einsum("bnhgqk,bnkhd->bnqhgd", p.astype(v.dtype), vv).astype(jnp.float32)
    o = o / jnp.moveaxis(tot, -1, 2)[..., None]
    o = o.astype(q.dtype).reshape(b, nb * blk, hq, d)[:, :L]
    lse = jnp.moveaxis(m + jnp.log(den), -1, 2).reshape(b, nb * blk, hq)[:, :L]
    return o, lse


def dilated_group(q, k, v, window, dilation):
    b, s, h, d = q.shape
    L = s // dilation

    def to_sub(t):
        return t.reshape(b, L, dilation, h, d).transpose(0, 2, 1, 3, 4).reshape(b * dilation, L, h, d)

    o, lse = banded_attention(to_sub(q), to_sub(k), to_sub(v), window // dilation)
    o = o.reshape(b, dilation, L, h, d).transpose(0, 2, 1, 3, 4).reshape(b, s, h, d)
    lse = lse.reshape(b, dilation, L, h).transpose(0, 2, 1, 3).reshape(b, s, h)
    return o, lse


def memory_attention(q, mem_n, w_mem_kv):
    b, n, _ = mem_n.shape
    mk, mv = jnp.split(mem_n @ w_mem_kv, 2, axis=-1)
    mk = mk.reshape(b, n, M_HEADS, HEAD_DIM)
    mv = mv.reshape(b, n, M_HEADS, HEAD_DIM)
    s = jnp.einsum("bshd,bmhd->bhsm", q, mk).astype(jnp.float32) * (HEAD_DIM ** -0.5)
    p = jax.nn.softmax(s, axis=-1)
    return jnp.einsum("bhsm,bmhd->bshd", p.astype(mv.dtype), mv)


def setup_inputs(seed: int = 0) -> dict:
    key = jax.random.key(seed)
    ks = jax.random.split(key, 24)
    f = jnp.float32

    def w(k, shape, fan_in):
        return jax.random.normal(k, (DEPTH,) + shape, f) * fan_in ** -0.5

    def gain(k):
        return 1.0 + 0.1 * jax.random.normal(k, (DEPTH, D_MODEL), f)

    return {
        "x": jax.random.normal(ks[0], (BATCH, SEQ, D_MODEL), f),
        "mem": jax.random.normal(ks[1], (BATCH, N_MEM, D_MODEL), f),
        "ffn1_norm_pre": gain(ks[2]),
        "ffn1_w_in": w(ks[3], (D_MODEL, 2 * D_FF), D_MODEL),
        "ffn1_w_out": w(ks[4], (D_FF, D_MODEL), D_FF),
        "ffn1_norm_post": gain(ks[5]),
        "mix_norm_pre": gain(ks[6]),
        "w_in": w(ks[7], (D_MODEL, D_IN), D_MODEL),
        "sinks": 0.5 * jax.random.normal(ks[8], (DEPTH, B_Q_HEADS), f),
        "mem_norm": gain(ks[9]),
        "w_mem_kv": w(ks[10], (D_MODEL, 2 * M_WIDTH), D_MODEL),
        "w_gate": w(ks[11], (D_MODEL, 3 * D_MODEL), D_MODEL),
        "b_gate": 0.01 * jax.random.normal(ks[12], (DEPTH, 3 * D_MODEL), f),
        "w_o_a": w(ks[13], (A_OUT, D_MODEL), A_OUT),
        "w_o_b": w(ks[14], (B_WIDTH, D_MODEL), B_WIDTH),
        "w_o_m": w(ks[15], (M_WIDTH, D_MODEL), M_WIDTH),
        "w_out": w(ks[16], (D_MODEL, D_MODEL), D_MODEL),
        "mix_norm_post": gain(ks[17]),
        "ffn2_norm_pre": gain(ks[18]),
        "ffn2_w_in": w(ks[19], (D_MODEL, 2 * D_FF), D_MODEL),
        "ffn2_w_out": w(ks[20], (D_FF, D_MODEL), D_FF),
        "ffn2_norm_post": gain(ks[21]),
    }


def reference(x, mem, ffn1_norm_pre, ffn1_w_in, ffn1_w_out, ffn1_norm_post, mix_norm_pre,
              w_in, sinks, mem_norm, w_mem_kv, w_gate, b_gate, w_o_a, w_o_b, w_o_m, w_out,
              mix_norm_post, ffn2_norm_pre, ffn2_w_in, ffn2_w_out, ffn2_norm_post):
    b, s, _ = x.shape
    pos = jnp.arange(s)
    h = x
    for l in range(DEPTH):
        f1 = swiglu(rms_norm(h, ffn1_norm_pre[l]), ffn1_w_in[l], ffn1_w_out[l])
        h = h + 0.5 * rms_norm(f1, ffn1_norm_post[l])

        u = rms_norm(h, mix_norm_pre[l])
        aq, ak, av, bq, bk, bv, mq = jnp.split(u @ w_in[l], IN_SPLITS, axis=-1)

        aq = rope(aq.reshape(b, s, A_HEADS, HEAD_DIM), pos).reshape(b, s, len(DIL_PAIRS), A_HEADS_PER_GROUP, HEAD_DIM)
        ak = rope(ak.reshape(b, s, A_HEADS, HEAD_DIM), pos).reshape(b, s, len(DIL_PAIRS), A_HEADS_PER_GROUP, HEAD_DIM)
        av = av.reshape(b, s, len(DIL_PAIRS), A_HEADS_PER_GROUP, HEAD_DIM)
        outs, lses = [], []
        for g, (window, dilation) in enumerate(DIL_PAIRS):
            o_g, lse_g = dilated_group(aq[:, :, g], ak[:, :, g], av[:, :, g], window, dilation)
            outs.append(o_g)
            lses.append(lse_g)
        wts = jax.nn.softmax(jnp.stack(lses, axis=0), axis=0)
        o_a = jnp.sum(wts[..., None] * jnp.stack(outs, axis=0).astype(jnp.float32), axis=0)
        o_a = o_a.astype(x.dtype).reshape(b, s, A_OUT)

        bq = rope(bq.reshape(b, s, B_Q_HEADS, HEAD_DIM), pos)
        bk = rope(bk.reshape(b, s, B_KV_HEADS, HEAD_DIM), pos)
        bv = bv.reshape(b, s, B_KV_HEADS, HEAD_DIM)
        o_b, _ = banded_attention(bq, bk, bv, B_WINDOW - 1, sink=sinks[l])
        o_b = o_b.reshape(b, s, B_WIDTH)

        o_m = memory_attention(mq.reshape(b, s, M_HEADS, HEAD_DIM), rms_norm(mem, mem_norm[l]), w_mem_kv[l])
        o_m = o_m.reshape(b, s, M_WIDTH)

        g_a, g_b, g_m = jnp.split(jax.nn.sigmoid(u @ w_gate[l] + b_gate[l]), 3, axis=-1)
        merged = g_a * (o_a @ w_o_a[l]) + g_b * (o_b @ w_o_b[l]) + g_m * (o_m @ w_o_m[l])
        h = h + rms_norm(merged @ w_out[l], mix_norm_post[l])

        f2 = swiglu(rms_norm(h, ffn2_norm_pre[l]), ffn2_w_in[l], ffn2_w_out[l])
        h = h + 0.5 * rms_norm(f2, ffn2_norm_post[l])
    return h
```

```python
import functools

import jax
import jax.numpy as jnp
from jax import lax
from jax.experimental import pallas as pl
from jax.experimental.pallas import tpu as pltpu

D_MODEL = 1024
HEAD_DIM = 128
DIL_PAIRS = ((128, 1), (512, 4), (2048, 16))
A_HEADS_PER_GROUP = 2
N_GROUPS = len(DIL_PAIRS)
A_HEADS = A_HEADS_PER_GROUP * N_GROUPS
B_Q_HEADS = 4
B_KV_HEADS = 2
B_WINDOW = 128
M_HEADS = 4
D_FF = 2816
ROPE_THETA = 10000.0
BLOCK = 128
EPS = 1e-6
NEG_INF = -1e30

A_WIDTH = A_HEADS * HEAD_DIM
A_OUT = A_HEADS_PER_GROUP * HEAD_DIM
B_WIDTH = B_Q_HEADS * HEAD_DIM
B_KV_WIDTH = B_KV_HEADS * HEAD_DIM
M_WIDTH = M_HEADS * HEAD_DIM
D_IN = 3 * A_WIDTH + B_WIDTH + 2 * B_KV_WIDTH + M_WIDTH
OFF_AQ = 0
OFF_AK = A_WIDTH
OFF_AV = 2 * A_WIDTH
OFF_BQ = 3 * A_WIDTH
OFF_BK = OFF_BQ + B_WIDTH
OFF_BV = OFF_BK + B_KV_WIDTH
OFF_MQ = OFF_BV + B_KV_WIDTH
QK_SCALE = HEAD_DIM ** -0.5

MXU_N = 256
ROW_TILE = 512
ATT_ROWS = 1024
VMEM_LIMIT = 56 * 1024 * 1024

F32 = jnp.float32
BF16 = jnp.bfloat16
NT_DIMS = (((1,), (1,)), ((), ()))


def _rms(x, g):
    return x * lax.rsqrt(jnp.mean(x * x, axis=-1, keepdims=True) + EPS) * g


def _const_spec(shape):
    nd = len(shape)
    return pl.BlockSpec(shape, lambda *_: (0,) * nd, pipeline_mode=pl.Buffered(1))


def _params(n_axes=1):
    return pltpu.CompilerParams(
        dimension_semantics=("arbitrary",) * n_axes, vmem_limit_bytes=VMEM_LIMIT)


def _mem_kv_kernel(mem_ref, g_ref, w_ref, mk_ref, mv_ref):
    mn = _rms(mem_ref[0], g_ref[...]).astype(BF16)
    kv = jnp.dot(mn, w_ref[...], preferred_element_type=F32)
    mk_ref[0] = kv[:, :M_WIDTH].astype(BF16)
    mv_ref[0] = kv[:, M_WIDTH:].astype(BF16)


def _mem_kv(mem, g, w):
    b, n, _ = mem.shape
    out = jax.ShapeDtypeStruct((b, n, M_WIDTH), BF16)
    return pl.pallas_call(
        _mem_kv_kernel,
        out_shape=(out, out),
        grid=(b,),
        in_specs=[pl.BlockSpec((1, n, D_MODEL), lambda i: (i, 0, 0)),
                  _const_spec((1, D_MODEL)),
                  _const_spec((D_MODEL, 2 * M_WIDTH))],
        out_specs=(pl.BlockSpec((1, n, M_WIDTH), lambda i: (i, 0, 0)),
                   pl.BlockSpec((1, n, M_WIDTH), lambda i: (i, 0, 0))),
        compiler_params=_params(),
        name="mem_kv",
    )(mem, g, w)


def _ffn_kernel(x_ref, gpre_ref, win_ref, wout_ref, gpost_ref, *rest, emit_u):
    if emit_u:
        gnext_ref, h_ref, u_ref, act_ref = rest
    else:
        h_ref, act_ref = rest
    x = x_ref[...]
    xn = _rms(x, gpre_ref[...]).astype(BF16)
    for c in range(D_FF // MXU_N):
        lo = c * MXU_N
        gate = jnp.dot(xn, win_ref[:, lo:lo + MXU_N], preferred_element_type=F32)
        up = jnp.dot(xn, win_ref[:, D_FF + lo:D_FF + lo + MXU_N], preferred_element_type=F32)
        silu = gate * (1.0 / (1.0 + jnp.exp(-gate)))
        act_ref[:, lo:lo + MXU_N] = (silu * up).astype(BF16)
    f = jnp.dot(act_ref[...], wout_ref[...], preferred_element_type=F32)
    h = x + 0.5 * _rms(f, gpost_ref[...])
    h_ref[...] = h
    if emit_u:
        u_ref[...] = _rms(h, gnext_ref[...]).astype(BF16)


def _ffn(x, gpre, w_in, w_out, gpost, gnext=None):
    t = x.shape[0]
    emit_u = gnext is not None
    row = pl.BlockSpec((ROW_TILE, D_MODEL), lambda i: (i, 0))
    gain = _const_spec((1, D_MODEL))
    in_specs = [row, gain, _const_spec((D_MODEL, 2 * D_FF)), _const_spec((D_FF, D_MODEL)), gain]
    args = [x, gpre, w_in, w_out, gpost]
    out_shape = [jax.ShapeDtypeStruct((t, D_MODEL), F32)]
    out_specs = [row]
    if emit_u:
        in_specs.append(gain)
        args.append(gnext)
        out_shape.append(jax.ShapeDtypeStruct((t, D_MODEL), BF16))
        out_specs.append(row)
    return pl.pallas_call(
        functools.partial(_ffn_kernel, emit_u=emit_u),
        out_shape=tuple(out_shape),
        grid=(t // ROW_TILE,),
        in_specs=in_specs,
        out_specs=tuple(out_specs),
        scratch_shapes=[pltpu.VMEM((ROW_TILE, D_FF), BF16)],
        compiler_params=_params(),
        name="ffn_u" if emit_u else "ffn",
    )(*args)


def _proj_kernel(u_ref, w_ref, cq_ref, sq_ref, ck_ref, sk_ref, mk_ref, mv_ref,
                 aq_ref, ak_ref, av_ref, bq_ref, bk_ref, bv_ref, om_ref):
    u = u_ref[...]

    def proj(lo):
        return jnp.dot(u, w_ref[:, lo:lo + MXU_N], preferred_element_type=F32)

    def rope_to(dst, off, width, cos, sin):
        for c in range(width // MXU_N):
            y = proj(off + c * MXU_N)
            for hh in range(MXU_N // HEAD_DIM):
                yh = y[:, hh * HEAD_DIM:(hh + 1) * HEAD_DIM]
                r = yh * cos + pltpu.roll(yh, HEAD_DIM // 2, 1) * sin
                col = c * MXU_N + hh * HEAD_DIM
                dst[:, col:col + HEAD_DIM] = r.astype(BF16)

    def plain_to(dst, off, width):
        for c in range(width // MXU_N):
            dst[:, c * MXU_N:(c + 1) * MXU_N] = proj(off + c * MXU_N).astype(BF16)

    cq, sq = cq_ref[...], sq_ref[...]
    ck, sk = ck_ref[...], sk_ref[...]
    rope_to(aq_ref, OFF_AQ, A_WIDTH, cq, sq)
    rope_to(ak_ref, OFF_AK, A_WIDTH, ck, sk)
    plain_to(av_ref, OFF_AV, A_WIDTH)
    rope_to(bq_ref, OFF_BQ, B_WIDTH, cq, sq)
    rope_to(bk_ref, OFF_BK, B_KV_WIDTH, ck, sk)
    plain_to(bv_ref, OFF_BV, B_KV_WIDTH)

    for c in range(M_WIDTH // MXU_N):
        y = proj(OFF_MQ + c * MXU_N) * QK_SCALE
        for hh in range(MXU_N // HEAD_DIM):
            h = c * (MXU_N // HEAD_DIM) + hh
            cols = slice(h * HEAD_DIM, (h + 1) * HEAD_DIM)
            q = y[:, hh * HEAD_DIM:(hh + 1) * HEAD_DIM].astype(BF16)
            s = lax.dot_general(q, mk_ref[0, :, cols], NT_DIMS, preferred_element_type=F32)
            m = jnp.max(s, axis=-1, keepdims=True)
            p = jnp.exp(s - m)
            den = jnp.sum(p, axis=-1, keepdims=True)
            o = jnp.dot(p.astype(BF16), mv_ref[0, :, cols], preferred_element_type=F32)
            om_ref[:, cols] = (o / den).astype(BF16)


def _proj(u, w_in, tables, mk, mv, seq):
    t = u.shape[0]
    n_mem = mk.shape[1]
    tiles_per_seq = seq // ROW_TILE

    def row(width):
        return pl.BlockSpec((ROW_TILE, width), lambda i: (i, 0))

    table = pl.BlockSpec((ROW_TILE, HEAD_DIM), lambda i: (i % tiles_per_seq, 0))
    memb = pl.BlockSpec((1, n_mem, M_WIDTH), lambda i: (i // tiles_per_seq, 0, 0))
    widths = (A_WIDTH, A_WIDTH, A_WIDTH, B_WIDTH, B_KV_WIDTH, B_KV_WIDTH, M_WIDTH)
    return pl.pallas_call(
        _proj_kernel,
        out_shape=tuple(jax.ShapeDtypeStruct((t, w), BF16) for w in widths),
        grid=(t // ROW_TILE,),
        in_specs=[row(D_MODEL), _const_spec((D_MODEL, D_IN)), table, table, table, table, memb, memb],
        out_specs=tuple(row(w) for w in widths),
        compiler_params=_params(),
        name="proj",
    )(u, w_in, *tables, mk, mv)


def _band_kernel(*refs, hq, hkv, max_dist, n_blk, with_sink, with_lse):
    refs = list(refs)
    sink_ref = refs.pop(0) if with_sink else None
    q_ref, k_ref, v_ref, kp_ref, vp_ref = refs[:5]
    o_ref = refs[5]
    lse_ref = refs[6] if with_lse else None
    grp = hq // hkv
    first_tile = pl.program_id(1) == 0

    row = lax.broadcasted_iota(jnp.int32, (BLOCK, 2 * BLOCK), 0)
    col = lax.broadcasted_iota(jnp.int32, (BLOCK, 2 * BLOCK), 1)
    dist = row + BLOCK - col
    band = (dist >= 0) & (dist <= max_dist)
    band_first = band & ((col >= BLOCK) | jnp.logical_not(first_tile))

    for j in range(n_blk):
        rows = slice(j * BLOCK, (j + 1) * BLOCK)
        mask = band_first if j == 0 else band
        for hk in range(hkv):
            kcols = slice(hk * HEAD_DIM, (hk + 1) * HEAD_DIM)
            if j == 0:
                kk = jnp.concatenate([kp_ref[0, :, kcols], k_ref[0, rows, kcols]], axis=0)
                vv = jnp.concatenate([vp_ref[0, :, kcols], v_ref[0, rows, kcols]], axis=0)
            else:
                both = slice((j - 1) * BLOCK, (j + 1) * BLOCK)
                kk = k_ref[0, both, kcols]
                vv = v_ref[0, both, kcols]
            for g in range(grp):
                h = hk * grp + g
                qcols = slice(h * HEAD_DIM, (h + 1) * HEAD_DIM)
                s = lax.dot_general(q_ref[0, rows, qcols], kk, NT_DIMS, preferred_element_type=F32)
                s = jnp.where(mask, s, NEG_INF)
                m = jnp.max(s, axis=-1, keepdims=True)
                if with_sink:
                    sk = sink_ref[h]
                    m = jnp.maximum(m, sk)
                p = jnp.exp(s - m)
                den = jnp.sum(p, axis=-1, keepdims=True)
                tot = den + jnp.exp(sk - m) if with_sink else den
                o = jnp.dot(p.astype(BF16), vv, preferred_element_type=F32)
                o_ref[0, rows, qcols] = (o / tot).astype(BF16)
                if with_lse:
                    lse_ref[0, rows, qcols] = jnp.broadcast_to(m + jnp.log(den), (BLOCK, HEAD_DIM))


def _band_attention(q, k, v, max_dist, sink=None, with_lse=False):
    n, length, qw = q.shape
    kw = k.shape[2]
    rows = min(ATT_ROWS, length)
    n_blk = rows // BLOCK
    cur = lambda w: pl.BlockSpec((1, rows, w), lambda b, i: (b, i, 0))
    prev = pl.BlockSpec((1, BLOCK, kw), lambda b, i: (b, jnp.maximum(i * n_blk - 1, 0), 0))
    in_specs = [cur(qw), cur(kw), cur(kw), prev, prev]
    args = [q, k, v, k, v]
    if sink is not None:
        in_specs.insert(0, pl.BlockSpec(memory_space=pltpu.SMEM))
        args.insert(0, sink)
    out_shape = [jax.ShapeDtypeStruct((n, length, qw), BF16)]
    out_specs = [cur(qw)]
    if with_lse:
        out_shape.append(jax.ShapeDtypeStruct((n, length, qw), F32))
        out_specs.append(cur(qw))
    return pl.pallas_call(
        functools.partial(_band_kernel, hq=qw // HEAD_DIM, hkv=kw // HEAD_DIM, max_dist=max_dist,
                          n_blk=n_blk, with_sink=sink is not None, with_lse=with_lse),
        out_shape=tuple(out_shape),
        grid=(n, length // rows),
        in_specs=in_specs,
        out_specs=tuple(out_specs),
        compiler_params=_params(2),
        name="band_sink" if sink is not None else "band_lse",
    )(*args)


def _merge_kernel(h_ref, u_ref, oa0_ref, oa1_ref, oa2_ref, l0_ref, l1_ref, l2_ref, ob_ref, om_ref,
                  wg_ref, bg_ref, woa_ref, wob_ref, wom_ref, wout_ref, gpost_ref, out_ref):
    u = u_ref[...]

    def gate(idx):
        cols = slice(idx * D_MODEL, (idx + 1) * D_MODEL)
        z = jnp.dot(u, wg_ref[:, cols], preferred_element_type=F32) + bg_ref[:, cols]
        return 1.0 / (1.0 + jnp.exp(-z))

    heads = []
    for hh in range(A_HEADS_PER_GROUP):
        cols = slice(hh * HEAD_DIM, (hh + 1) * HEAD_DIM)
        l0, l1, l2 = l0_ref[:, cols], l1_ref[:, cols], l2_ref[:, cols]
        mx = jnp.maximum(jnp.maximum(l0, l1), l2)
        e0, e1, e2 = jnp.exp(l0 - mx), jnp.exp(l1 - mx), jnp.exp(l2 - mx)
        den = e0 + e1 + e2
        acc = ((e0 / den) * oa0_ref[:, cols].astype(F32)
               + (e1 / den) * oa1_ref[:, cols].astype(F32)
               + (e2 / den) * oa2_ref[:, cols].astype(F32))
        heads.append(acc.astype(BF16))
    o_a = jnp.concatenate(heads, axis=1)

    merged = gate(0) * jnp.dot(o_a, woa_ref[...], preferred_element_type=F32)
    merged = merged + gate(1) * jnp.dot(ob_ref[...], wob_ref[...], preferred_element_type=F32)
    merged = merged + gate(2) * jnp.dot(om_ref[...], wom_ref[...], preferred_element_type=F32)
    mixed = jnp.dot(merged.astype(BF16), wout_ref[...], preferred_element_type=F32)
    out_ref[...] = h_ref[...] + _rms(mixed, gpost_ref[...])


def _merge(h, u, oa, lse, ob, om, wg, bg, woa, wob, wom, wout, gpost):
    t = h.shape[0]

    def row(width):
        return pl.BlockSpec((ROW_TILE, width), lambda i: (i, 0))

    in_specs = ([row(D_MODEL), row(D_MODEL)] + [row(A_OUT)] * 6 + [row(B_WIDTH), row(M_WIDTH)]
                + [_const_spec((D_MODEL, 3 * D_MODEL)), _const_spec((1, 3 * D_MODEL)),
                   _const_spec((A_OUT, D_MODEL)), _const_spec((B_WIDTH, D_MODEL)),
                   _const_spec((M_WIDTH, D_MODEL)), _const_spec((D_MODEL, D_MODEL)),
                   _const_spec((1, D_MODEL))])
    return pl.pallas_call(
        _merge_kernel,
        out_shape=jax.ShapeDtypeStruct((t, D_MODEL), F32),
        grid=(t // ROW_TILE,),
        in_specs=in_specs,
        out_specs=row(D_MODEL),
        compiler_params=_params(),
        name="merge",
    )(h, u, *oa, *lse, ob, om, wg, bg, woa, wob, wom, wout, gpost)


def _rope_tables(seq):
    half = HEAD_DIM // 2
    inv = ROPE_THETA ** (-jnp.arange(half, dtype=F32) / half)
    ang = jnp.arange(seq).astype(F32)[:, None] * inv[None, :]
    cos, sin = jnp.cos(ang), jnp.sin(ang)
    cos = jnp.concatenate([cos, cos], axis=-1)
    sin = jnp.concatenate([-sin, sin], axis=-1)
    return cos * QK_SCALE, sin * QK_SCALE, cos, sin


def _to_sub(t, dilation):
    n, s, w = t.shape
    return (t.reshape(n, s // dilation, dilation, w).transpose(0, 2, 1, 3)
            .reshape(n * dilation, s // dilation, w))


def _from_sub(t, dilation):
    nd, length, w = t.shape
    n = nd // dilation
    return (t.reshape(n, dilation, length, w).transpose(0, 2, 1, 3)
            .reshape(n, length * dilation, w))


def kernel(x, mem, ffn1_norm_pre, ffn1_w_in, ffn1_w_out, ffn1_norm_post, mix_norm_pre, w_in, sinks, mem_norm, w_mem_kv, w_gate, b_gate, w_o_a, w_o_b, w_o_m, w_out, mix_norm_post, ffn2_norm_pre, ffn2_w_in, ffn2_w_out, ffn2_norm_post):
    b, s, _ = x.shape
    depth = ffn1_w_in.shape[0]
    tables = _rope_tables(s)
    h = x.reshape(b * s, D_MODEL)
    for l in range(depth):
        bf = lambda w: w[l].astype(BF16)
        mk, mv = _mem_kv(mem, mem_norm[l][None], bf(w_mem_kv))
        h1, u = _ffn(h, ffn1_norm_pre[l][None], bf(ffn1_w_in), bf(ffn1_w_out),
                     ffn1_norm_post[l][None], mix_norm_pre[l][None])
        aq, ak, av, bq, bk, bv, om = _proj(u, bf(w_in), tables, mk, mv, s)

        seq3 = lambda t: t.reshape(b, s, t.shape[-1])
        aq, ak, av = seq3(aq), seq3(ak), seq3(av)
        oa, lse = [], []
        for g, (window, dilation) in enumerate(DIL_PAIRS):
            cols = slice(g * A_OUT, (g + 1) * A_OUT)
            qg, kg, vg = aq[..., cols], ak[..., cols], av[..., cols]
            if dilation > 1:
                qg, kg, vg = _to_sub(qg, dilation), _to_sub(kg, dilation), _to_sub(vg, dilation)
            o_g, lse_g = _band_attention(qg, kg, vg, window // dilation, with_lse=True)
            if dilation > 1:
                o_g, lse_g = _from_sub(o_g, dilation), _from_sub(lse_g, dilation)
            oa.append(o_g.reshape(b * s, A_OUT))
            lse.append(lse_g.reshape(b * s, A_OUT))
        (ob,) = _band_attention(seq3(bq), seq3(bk), seq3(bv), B_WINDOW - 1, sink=sinks[l])
        ob = ob.reshape(b * s, B_WIDTH)

        h2 = _merge(h1, u, oa, lse, ob, om, bf(w_gate), b_gate[l][None], bf(w_o_a), bf(w_o_b),
                    bf(w_o_m), bf(w_out), mix_norm_post[l][None])
        (h,) = _ffn(h2, ffn2_norm_pre[l][None], bf(ffn2_w_in), bf(ffn2_w_out), ffn2_norm_post[l][None])
    return h.reshape(b, s, D_MODEL)
```

```python
import functools

import jax
import jax.numpy as jnp
import numpy as np
from jax import lax
from jax.experimental import pallas as pl
from jax.experimental.pallas import tpu as pltpu

D_MODEL = 1024
HEAD_DIM = 128
DIL_PAIRS = ((128, 1), (512, 4), (2048, 16))
A_HEADS_PER_GROUP = 2
N_GROUPS = len(DIL_PAIRS)
A_HEADS = A_HEADS_PER_GROUP * N_GROUPS
B_Q_HEADS = 4
B_KV_HEADS = 2
B_WINDOW = 128
M_HEADS = 4
D_FF = 2816
ROPE_THETA = 10000.0
BLOCK = 128
EPS = 1e-6
NEG_INF = -1e30

A_WIDTH = A_HEADS * HEAD_DIM
A_OUT = A_HEADS_PER_GROUP * HEAD_DIM
B_WIDTH = B_Q_HEADS * HEAD_DIM
B_KV_WIDTH = B_KV_HEADS * HEAD_DIM
M_WIDTH = M_HEADS * HEAD_DIM
D_IN = 3 * A_WIDTH + B_WIDTH + 2 * B_KV_WIDTH + M_WIDTH
OFF_AQ = 0
OFF_AK = A_WIDTH
OFF_AV = 2 * A_WIDTH
OFF_BQ = 3 * A_WIDTH
OFF_BK = OFF_BQ + B_WIDTH
OFF_BV = OFF_BK + B_KV_WIDTH
OFF_MQ = OFF_BV + B_KV_WIDTH
QK_SCALE = HEAD_DIM ** -0.5

MXU_N = 256
ROW_TILE = 512
ATT_ROWS = 1024
VMEM_LIMIT = 56 * 1024 * 1024

MAX_DIL = max(d for _, d in DIL_PAIRS)
CHUNK = MAX_DIL * BLOCK
TILES_PER_CHUNK = CHUNK // ROW_TILE
TILE_RUN = ROW_TILE // MAX_DIL
A12_WIDTH = 2 * A_OUT

F32 = jnp.float32
BF16 = jnp.bfloat16
NT_DIMS = (((1,), (1,)), ((), ()))


def _rms(x, g):
    return x * lax.rsqrt(jnp.mean(x * x, axis=-1, keepdims=True) + EPS) * g


def _const_spec(shape):
    nd = len(shape)
    return pl.BlockSpec(shape, lambda *_: (0,) * nd, pipeline_mode=pl.Buffered(1))


def _params(n_axes=1):
    return pltpu.CompilerParams(
        dimension_semantics=("arbitrary",) * n_axes, vmem_limit_bytes=VMEM_LIMIT)


def _mem_kv_kernel(mem_ref, g_ref, w_ref, mk_ref, mv_ref):
    mn = _rms(mem_ref[0], g_ref[...]).astype(BF16)
    kv = jnp.dot(mn, w_ref[...], preferred_element_type=F32)
    mk_ref[0] = kv[:, :M_WIDTH].astype(BF16)
    mv_ref[0] = kv[:, M_WIDTH:].astype(BF16)


def _mem_kv(mem, g, w):
    b, n, _ = mem.shape
    out = jax.ShapeDtypeStruct((b, n, M_WIDTH), BF16)
    return pl.pallas_call(
        _mem_kv_kernel,
        out_shape=(out, out),
        grid=(b,),
        in_specs=[pl.BlockSpec((1, n, D_MODEL), lambda i: (i, 0, 0)),
                  _const_spec((1, D_MODEL)),
                  _const_spec((D_MODEL, 2 * M_WIDTH))],
        out_specs=(pl.BlockSpec((1, n, M_WIDTH), lambda i: (i, 0, 0)),
                   pl.BlockSpec((1, n, M_WIDTH), lambda i: (i, 0, 0))),
        compiler_params=_params(),
        name="mem_kv",
    )(mem, g, w)


def _ffn_kernel(x_ref, gpre_ref, win_ref, wout_ref, gpost_ref, *rest, emit_u):
    if emit_u:
        gnext_ref, h_ref, u_ref, act_ref = rest
    else:
        h_ref, act_ref = rest
    x = x_ref[...]
    xn = _rms(x, gpre_ref[...]).astype(BF16)
    for c in range(D_FF // MXU_N):
        lo = c * MXU_N
        gate = jnp.dot(xn, win_ref[:, lo:lo + MXU_N], preferred_element_type=F32)
        up = jnp.dot(xn, win_ref[:, D_FF + lo:D_FF + lo + MXU_N], preferred_element_type=F32)
        silu = gate * (1.0 / (1.0 + jnp.exp(-gate)))
        act_ref[:, lo:lo + MXU_N] = (silu * up).astype(BF16)
    f = jnp.dot(act_ref[...], wout_ref[...], preferred_element_type=F32)
    h = x + 0.5 * _rms(f, gpost_ref[...])
    h_ref[...] = h
    if emit_u:
        u_ref[...] = _rms(h, gnext_ref[...]).astype(BF16)


def _ffn(x, gpre, w_in, w_out, gpost, gnext=None):
    t = x.shape[0]
    emit_u = gnext is not None
    row = pl.BlockSpec((ROW_TILE, D_MODEL), lambda i: (i, 0))
    gain = _const_spec((1, D_MODEL))
    in_specs = [row, gain, _const_spec((D_MODEL, 2 * D_FF)), _const_spec((D_FF, D_MODEL)), gain]
    args = [x, gpre, w_in, w_out, gpost]
    out_shape = [jax.ShapeDtypeStruct((t, D_MODEL), F32)]
    out_specs = [row]
    if emit_u:
        in_specs.append(gain)
        args.append(gnext)
        out_shape.append(jax.ShapeDtypeStruct((t, D_MODEL), BF16))
        out_specs.append(row)
    return pl.pallas_call(
        functools.partial(_ffn_kernel, emit_u=emit_u),
        out_shape=tuple(out_shape),
        grid=(t // ROW_TILE,),
        in_specs=in_specs,
        out_specs=tuple(out_specs),
        scratch_shapes=[pltpu.VMEM((ROW_TILE, D_FF), BF16)],
        compiler_params=_params(),
        name="ffn_u" if emit_u else "ffn",
    )(*args)


def _proj_kernel(u_ref, w_ref, perm_ref, cq_ref, sq_ref, ck_ref, sk_ref, cqp_ref, sqp_ref, ckp_ref,
                 skp_ref, mk_ref, mv_ref,
                 q0_ref, k0_ref, v0_ref, q12_ref, k12_ref, v12_ref, bq_ref, bk_ref, bv_ref, om_ref):
    u = u_ref[...]
    u_res = jnp.dot(perm_ref[...], u, preferred_element_type=F32).astype(BF16)

    def proj(lhs, lo):
        return jnp.dot(lhs, w_ref[:, lo:lo + MXU_N], preferred_element_type=F32)

    def rope(y, cos, sin):
        out = []
        for hh in range(MXU_N // HEAD_DIM):
            yh = y[:, hh * HEAD_DIM:(hh + 1) * HEAD_DIM]
            out.append(yh * cos + pltpu.roll(yh, HEAD_DIM // 2, 1) * sin)
        return jnp.concatenate(out, axis=1)

    def put_res(dst, col, y):
        for c in range(MAX_DIL):
            dst[c, :, col:col + MXU_N] = y[c * TILE_RUN:(c + 1) * TILE_RUN].astype(dst.dtype)

    cqp, sqp, ckp, skp = cqp_ref[...], sqp_ref[...], ckp_ref[...], skp_ref[...]
    for g in range(N_GROUPS):
        dst_q, dst_k, dst_v = (q0_ref, k0_ref, v0_ref) if g == 0 else (q12_ref, k12_ref, v12_ref)
        col = 0 if g == 0 else (g - 1) * A_OUT
        put_res(dst_q, col, rope(proj(u_res, OFF_AQ + g * A_OUT), cqp, sqp))
        put_res(dst_k, col, rope(proj(u_res, OFF_AK + g * A_OUT), ckp, skp))
        put_res(dst_v, col, proj(u_res, OFF_AV + g * A_OUT))

    cq, sq, ck, sk = cq_ref[...], sq_ref[...], ck_ref[...], sk_ref[...]
    for c in range(B_WIDTH // MXU_N):
        bq_ref[:, c * MXU_N:(c + 1) * MXU_N] = rope(proj(u, OFF_BQ + c * MXU_N), cq, sq).astype(BF16)
    bk_ref[...] = rope(proj(u, OFF_BK), ck, sk).astype(BF16)
    bv_ref[...] = proj(u, OFF_BV).astype(BF16)

    for c in range(M_WIDTH // MXU_N):
        y = proj(u, OFF_MQ + c * MXU_N) * QK_SCALE
        for hh in range(MXU_N // HEAD_DIM):
            h = c * (MXU_N // HEAD_DIM) + hh
            cols = slice(h * HEAD_DIM, (h + 1) * HEAD_DIM)
            q = y[:, hh * HEAD_DIM:(hh + 1) * HEAD_DIM].astype(BF16)
            s = lax.dot_general(q, mk_ref[0, :, cols], NT_DIMS, preferred_element_type=F32)
            m = jnp.max(s, axis=-1, keepdims=True)
            p = jnp.exp(s - m)
            den = jnp.sum(p, axis=-1, keepdims=True)
            o = jnp.dot(p.astype(BF16), mv_ref[0, :, cols], preferred_element_type=F32)
            om_ref[:, cols] = (o / den).astype(BF16)


def _res_spec(width):
    return pl.BlockSpec((None, MAX_DIL, None, TILE_RUN, width),
                        lambda i: (i // TILES_PER_CHUNK, 0, i % TILES_PER_CHUNK, 0, 0))


def _proj(u, w_in, perm, tables, tables_res, mk, mv, seq):
    t = u.shape[0]
    n_mem = mk.shape[1]
    tiles_per_seq = seq // ROW_TILE

    def row(width):
        return pl.BlockSpec((ROW_TILE, width), lambda i: (i, 0))

    def res_shape(width, dtype):
        return jax.ShapeDtypeStruct((t // CHUNK, MAX_DIL, TILES_PER_CHUNK, TILE_RUN, width), dtype)

    table = pl.BlockSpec((ROW_TILE, HEAD_DIM), lambda i: (i % tiles_per_seq, 0))
    memb = pl.BlockSpec((1, n_mem, M_WIDTH), lambda i: (i // tiles_per_seq, 0, 0))
    out_shape = ([res_shape(A_OUT, F32)] * 3 + [res_shape(A12_WIDTH, BF16)] * 3
                 + [jax.ShapeDtypeStruct((t, w), BF16) for w in (B_WIDTH, B_KV_WIDTH, B_KV_WIDTH, M_WIDTH)])
    out_specs = ([_res_spec(A_OUT)] * 3 + [_res_spec(A12_WIDTH)] * 3
                 + [row(w) for w in (B_WIDTH, B_KV_WIDTH, B_KV_WIDTH, M_WIDTH)])
    return pl.pallas_call(
        _proj_kernel,
        out_shape=tuple(out_shape),
        grid=(t // ROW_TILE,),
        in_specs=[row(D_MODEL), _const_spec((D_MODEL, D_IN)), _const_spec((ROW_TILE, ROW_TILE))]
        + [table] * 8 + [memb, memb],
        out_specs=tuple(out_specs),
        compiler_params=_params(),
        name="proj",
    )(u, w_in, perm, *tables, *tables_res, mk, mv)


def _band_kernel(sink_ref, q_ref, k_ref, v_ref, kp_ref, vp_ref, o_ref, *, hq, hkv, max_dist, n_blk):
    grp = hq // hkv
    first_tile = pl.program_id(1) == 0

    row = lax.broadcasted_iota(jnp.int32, (BLOCK, 2 * BLOCK), 0)
    col = lax.broadcasted_iota(jnp.int32, (BLOCK, 2 * BLOCK), 1)
    dist = row + BLOCK - col
    band = (dist >= 0) & (dist <= max_dist)
    band_first = band & ((col >= BLOCK) | jnp.logical_not(first_tile))

    for j in range(n_blk):
        rows = slice(j * BLOCK, (j + 1) * BLOCK)
        mask = band_first if j == 0 else band
        for hk in range(hkv):
            kcols = slice(hk * HEAD_DIM, (hk + 1) * HEAD_DIM)
            if j == 0:
                kk = jnp.concatenate([kp_ref[0, :, kcols], k_ref[0, rows, kcols]], axis=0)
                vv = jnp.concatenate([vp_ref[0, :, kcols], v_ref[0, rows, kcols]], axis=0)
            else:
                both = slice((j - 1) * BLOCK, (j + 1) * BLOCK)
                kk = k_ref[0, both, kcols]
                vv = v_ref[0, both, kcols]
            for g in range(grp):
                h = hk * grp + g
                qcols = slice(h * HEAD_DIM, (h + 1) * HEAD_DIM)
                s = lax.dot_general(q_ref[0, rows, qcols], kk, NT_DIMS, preferred_element_type=F32)
                s = jnp.where(mask, s, NEG_INF)
                sk = sink_ref[h]
                m = jnp.maximum(jnp.max(s, axis=-1, keepdims=True), sk)
                p = jnp.exp(s - m)
                tot = jnp.sum(p, axis=-1, keepdims=True) + jnp.exp(sk - m)
                o = jnp.dot(p.astype(BF16), vv, preferred_element_type=F32)
                o_ref[0, rows, qcols] = (o / tot).astype(BF16)


def _band_attention(q, k, v, max_dist, sink):
    n, length, qw = q.shape
    kw = k.shape[2]
    rows = min(ATT_ROWS, length)
    n_blk = rows // BLOCK
    cur = lambda w: pl.BlockSpec((1, rows, w), lambda b, i: (b, i, 0))
    prev = pl.BlockSpec((1, BLOCK, kw), lambda b, i: (b, jnp.maximum(i * n_blk - 1, 0), 0))
    return pl.pallas_call(
        functools.partial(_band_kernel, hq=qw // HEAD_DIM, hkv=kw // HEAD_DIM, max_dist=max_dist,
                          n_blk=n_blk),
        out_shape=jax.ShapeDtypeStruct((n, length, qw), BF16),
        grid=(n, length // rows),
        in_specs=[pl.BlockSpec(memory_space=pltpu.SMEM), cur(qw), cur(kw), cur(kw), prev, prev],
        out_specs=cur(qw),
        compiler_params=_params(2),
        name="band_sink",
    )(sink, q, k, v, k, v)


def _dilated_kernel(q0_ref, k0_ref, v0_ref, k0p_ref, v0p_ref,
                    q12_ref, k12_ref, v12_ref, k1p_ref, v1p_ref, k2p_ref, v2p_ref,
                    o_ref, og_ref, lg_ref):
    first_chunk = pl.program_id(1) == 0
    row = lax.broadcasted_iota(jnp.int32, (BLOCK, 2 * BLOCK), 0)
    col = lax.broadcasted_iota(jnp.int32, (BLOCK, 2 * BLOCK), 1)
    own = col >= BLOCK
    kcol = jnp.where(own, col - BLOCK, col)

    def masks(n_pieces):
        piece = BLOCK // n_pieces
        shift = piece.bit_length() - 1
        pos = lambda r: (r & (piece - 1)) * n_pieces + (r >> shift)
        dist = pos(row) - pos(kcol) + jnp.where(own, 0, BLOCK)
        band = (dist >= 0) & (dist <= BLOCK)
        return band, band & (own | jnp.logical_not(first_chunk))

    def attend(q, kk, vv, mask):
        s = lax.dot_general(q, kk, NT_DIMS, preferred_element_type=F32)
        s = jnp.where(mask, s, NEG_INF)
        m = jnp.max(s, axis=-1, keepdims=True)
        p = jnp.exp(s - m)
        den = jnp.sum(p, axis=-1, keepdims=True)
        o = jnp.dot(p.astype(BF16), vv, preferred_element_type=F32) / den
        return o, jnp.broadcast_to(m + jnp.log(den), (BLOCK, HEAD_DIM))

    def gather(ref, starts, piece, cols):
        return jnp.concatenate([ref[s:s + piece, cols] for s in starts], axis=0).astype(BF16)

    for g, n_pieces in ((0, MAX_DIL), (1, MAX_DIL // DIL_PAIRS[1][1])):
        piece = BLOCK // n_pieces
        band, band_first = masks(n_pieces)
        q_ref, k_ref, v_ref, kp_ref, vp_ref = ((q0_ref, k0_ref, v0_ref, k0p_ref, v0p_ref) if g == 0 else
                                               (q12_ref, k12_ref, v12_ref, k1p_ref, v1p_ref))
        n_res = MAX_DIL // n_pieces
        for res in range(n_res):
            for h in range(A_HEADS_PER_GROUP):
                cols = slice(h * HEAD_DIM, (h + 1) * HEAD_DIM)
                bases = [(res + n_res * p) * BLOCK for p in range(n_pieces)]
                k_prev = jnp.concatenate([kp_ref[res + n_res * p, :, cols] for p in range(n_pieces)],
                                         axis=0).astype(BF16)
                v_prev = jnp.concatenate([vp_ref[res + n_res * p, :, cols] for p in range(n_pieces)],
                                         axis=0).astype(BF16)
                for a in range(BLOCK // piece):
                    starts = [b0 + a * piece for b0 in bases]
                    k_own = gather(k_ref, starts, piece, cols)
                    v_own = gather(v_ref, starts, piece, cols)
                    o, lse = attend(gather(q_ref, starts, piece, cols),
                                    jnp.concatenate([k_prev, k_own], axis=0),
                                    jnp.concatenate([v_prev, v_own], axis=0),
                                    band_first if a == 0 else band)
                    for p, s0 in enumerate(starts):
                        og_ref[g, s0:s0 + piece, cols] = o[p * piece:(p + 1) * piece]
                        lg_ref[g, s0:s0 + piece, cols] = lse[p * piece:(p + 1) * piece]
                    k_prev, v_prev = k_own, v_own

    band, band_first = masks(1)
    del band
    for c in range(MAX_DIL):
        rows = slice(c * BLOCK, (c + 1) * BLOCK)
        for h in range(A_HEADS_PER_GROUP):
            cols = slice(h * HEAD_DIM, (h + 1) * HEAD_DIM)
            cols2 = slice(A_OUT + h * HEAD_DIM, A_OUT + (h + 1) * HEAD_DIM)
            o2, l2 = attend(q12_ref[rows, cols2],
                            jnp.concatenate([k2p_ref[rows, cols], k12_ref[rows, cols2]], axis=0),
                            jnp.concatenate([v2p_ref[rows, cols], v12_ref[rows, cols2]], axis=0),
                            band_first)
            l0, l1 = lg_ref[0, rows, cols], lg_ref[1, rows, cols]
            mx = jnp.maximum(jnp.maximum(l0, l1), l2)
            e0, e1, e2 = jnp.exp(l0 - mx), jnp.exp(l1 - mx), jnp.exp(l2 - mx)
            den = e0 + e1 + e2
            acc = (e0 / den) * og_ref[0, rows, cols] + (e1 / den) * og_ref[1, rows, cols] + (e2 / den) * o2
            o_ref[rows, cols] = acc.astype(BF16)


def _dilated_attention(q0, k0, v0, q12, k12, v12, batch):
    t = q0.shape[0]
    n_chunks = t // CHUNK
    per_seq = n_chunks // batch
    chunk = lambda b, i: b * per_seq + i
    prev = lambda b, i: b * per_seq + jnp.maximum(i - 1, 0)

    def cur(width):
        return pl.BlockSpec((CHUNK, width), lambda b, i: (chunk(b, i), 0))

    def tail(piece, width, col):
        return pl.BlockSpec((None, MAX_DIL, None, piece, width),
                            lambda b, i: (prev(b, i), 0, BLOCK // piece - 1, 0, col))

    def tail_view(x, piece):
        return x.reshape(n_chunks, MAX_DIL, BLOCK // piece, piece, x.shape[-1])

    piece0 = BLOCK // MAX_DIL
    piece1 = BLOCK // (MAX_DIL // DIL_PAIRS[1][1])
    prev2 = pl.BlockSpec((None, CHUNK, A_OUT), lambda b, i: (prev(b, i), 0, 1))
    view2 = lambda x: x.reshape(n_chunks, CHUNK, A12_WIDTH)
    return pl.pallas_call(
        _dilated_kernel,
        out_shape=jax.ShapeDtypeStruct((t, A_OUT), BF16),
        grid=(batch, per_seq),
        in_specs=[cur(A_OUT), cur(A_OUT), cur(A_OUT), tail(piece0, A_OUT, 0), tail(piece0, A_OUT, 0),
                  cur(A12_WIDTH), cur(A12_WIDTH), cur(A12_WIDTH),
                  tail(piece1, A_OUT, 0), tail(piece1, A_OUT, 0), prev2, prev2],
        out_specs=cur(A_OUT),
        scratch_shapes=[pltpu.VMEM((2, CHUNK, A_OUT), F32), pltpu.VMEM((2, CHUNK, A_OUT), F32)],
        compiler_params=_params(2),
        name="dilated",
    )(q0, k0, v0, tail_view(k0, piece0), tail_view(v0, piece0),
      q12, k12, v12, tail_view(k12, piece1), tail_view(v12, piece1), view2(k12), view2(v12))


def _merge_kernel(h_ref, u_ref, oa_ref, ob_ref, om_ref, unperm_ref,
                  wg_ref, bg_ref, woa_ref, wob_ref, wom_ref, wout_ref, gpost_ref, out_ref):
    u = u_ref[...]

    def gate(idx):
        cols = slice(idx * D_MODEL, (idx + 1) * D_MODEL)
        z = jnp.dot(u, wg_ref[:, cols], preferred_element_type=F32) + bg_ref[:, cols]
        return 1.0 / (1.0 + jnp.exp(-z))

    oa_res = jnp.concatenate([oa_ref[c] for c in range(MAX_DIL)], axis=0)
    o_a = jnp.dot(unperm_ref[...], oa_res, preferred_element_type=F32).astype(BF16)

    merged = gate(0) * jnp.dot(o_a, woa_ref[...], preferred_element_type=F32)
    merged = merged + gate(1) * jnp.dot(ob_ref[...], wob_ref[...], preferred_element_type=F32)
    merged = merged + gate(2) * jnp.dot(om_ref[...], wom_ref[...], preferred_element_type=F32)
    mixed = jnp.dot(merged.astype(BF16), wout_ref[...], preferred_element_type=F32)
    out_ref[...] = h_ref[...] + _rms(mixed, gpost_ref[...])


def _merge(h, u, oa, ob, om, unperm, wg, bg, woa, wob, wom, wout, gpost):
    t = h.shape[0]

    def row(width):
        return pl.BlockSpec((ROW_TILE, width), lambda i: (i, 0))

    in_specs = ([row(D_MODEL), row(D_MODEL), _res_spec(A_OUT), row(B_WIDTH), row(M_WIDTH),
                 _const_spec((ROW_TILE, ROW_TILE)),
                 _const_spec((D_MODEL, 3 * D_MODEL)), _const_spec((1, 3 * D_MODEL)),
                 _const_spec((A_OUT, D_MODEL)), _const_spec((B_WIDTH, D_MODEL)),
                 _const_spec((M_WIDTH, D_MODEL)), _const_spec((D_MODEL, D_MODEL)),
                 _const_spec((1, D_MODEL))])
    oa = oa.reshape(t // CHUNK, MAX_DIL, TILES_PER_CHUNK, TILE_RUN, A_OUT)
    return pl.pallas_call(
        _merge_kernel,
        out_shape=jax.ShapeDtypeStruct((t, D_MODEL), F32),
        grid=(t // ROW_TILE,),
        in_specs=in_specs,
        out_specs=row(D_MODEL),
        compiler_params=_params(),
        name="merge",
    )(h, u, oa, ob, om, unperm, wg, bg, woa, wob, wom, wout, gpost)


def _rope_tables(seq):
    half = HEAD_DIM // 2
    inv = ROPE_THETA ** (-jnp.arange(half, dtype=F32) / half)
    ang = jnp.arange(seq).astype(F32)[:, None] * inv[None, :]
    cos, sin = jnp.cos(ang), jnp.sin(ang)
    cos = jnp.concatenate([cos, cos], axis=-1)
    sin = jnp.concatenate([-sin, sin], axis=-1)
    return cos * QK_SCALE, sin * QK_SCALE, cos, sin


def _tile_residue_major(table):
    s, w = table.shape
    return (table.reshape(s // ROW_TILE, TILE_RUN, MAX_DIL, w).transpose(0, 2, 1, 3).reshape(s, w))


def _residue_perm():
    out_row = np.arange(ROW_TILE)
    src = (out_row % TILE_RUN) * MAX_DIL + out_row // TILE_RUN
    return np.equal(src[:, None], np.arange(ROW_TILE)[None, :]).astype(np.float32)


def kernel(x, mem, ffn1_norm_pre, ffn1_w_in, ffn1_w_out, ffn1_norm_post, mix_norm_pre, w_in, sinks, mem_norm, w_mem_kv, w_gate, b_gate, w_o_a, w_o_b, w_o_m, w_out, mix_norm_post, ffn2_norm_pre, ffn2_w_in, ffn2_w_out, ffn2_norm_post):
    b, s, _ = x.shape
    depth = ffn1_w_in.shape[0]
    t = b * s
    tables = _rope_tables(s)
    tables_res = tuple(_tile_residue_major(tb) for tb in tables)
    perm = jnp.asarray(_residue_perm(), dtype=BF16)
    unperm = jnp.asarray(_residue_perm().T, dtype=BF16)
    h = x.reshape(t, D_MODEL)
    for l in range(depth):
        bf = lambda w: w[l].astype(BF16)
        mk, mv = _mem_kv(mem, mem_norm[l][None], bf(w_mem_kv))
        h1, u = _ffn(h, ffn1_norm_pre[l][None], bf(ffn1_w_in), bf(ffn1_w_out),
                     ffn1_norm_post[l][None], mix_norm_pre[l][None])
        q0, k0, v0, q12, k12, v12, bq, bk, bv, om = _proj(u, bf(w_in), perm, tables, tables_res, mk, mv, s)

        flat = lambda a: a.reshape(t, a.shape[-1])
        oa = _dilated_attention(flat(q0), flat(k0), flat(v0), flat(q12), flat(k12), flat(v12), b)
        seq3 = lambda a: a.reshape(b, s, a.shape[-1])
        ob = _band_attention(seq3(bq), seq3(bk), seq3(bv), B_WINDOW - 1, sinks[l]).reshape(t, B_WIDTH)

        h2 = _merge(h1, u, oa, ob, om, unperm, bf(w_gate), b_gate[l][None], bf(w_o_a), bf(w_o_b),
                    bf(w_o_m), bf(w_out), mix_norm_post[l][None])
        (h,) = _ffn(h2, ffn2_norm_pre[l][None], bf(ffn2_w_in), bf(ffn2_w_out), ffn2_norm_post[l][None])
    return h.reshape(b, s, D_MODEL)
```

```python
import functools

import jax
import jax.numpy as jnp
import numpy as np
from jax import lax
from jax.experimental import pallas as pl
from jax.experimental.pallas import tpu as pltpu

D_MODEL = 1024
HEAD_DIM = 128
DIL_PAIRS = ((128, 1), (512, 4), (2048, 16))
A_HEADS_PER_GROUP = 2
N_GROUPS = len(DIL_PAIRS)
A_HEADS = A_HEADS_PER_GROUP * N_GROUPS
B_Q_HEADS = 4
B_KV_HEADS = 2
B_WINDOW = 128
M_HEADS = 4
D_FF = 2816
ROPE_THETA = 10000.0
BLOCK = 128
EPS = 1e-6
NEG_INF = -1e30

A_WIDTH = A_HEADS * HEAD_DIM
A_OUT = A_HEADS_PER_GROUP * HEAD_DIM
B_WIDTH = B_Q_HEADS * HEAD_DIM
B_KV_WIDTH = B_KV_HEADS * HEAD_DIM
M_WIDTH = M_HEADS * HEAD_DIM
D_IN = 3 * A_WIDTH + B_WIDTH + 2 * B_KV_WIDTH + M_WIDTH
OFF_AQ = 0
OFF_AK = A_WIDTH
OFF_AV = 2 * A_WIDTH
OFF_BQ = 3 * A_WIDTH
OFF_BK = OFF_BQ + B_WIDTH
OFF_BV = OFF_BK + B_KV_WIDTH
OFF_MQ = OFF_BV + B_KV_WIDTH
QK_SCALE = HEAD_DIM ** -0.5

MXU_N = 256
ROW_TILE = 512
STEP_TILES = 2
STEP_ROWS = STEP_TILES * ROW_TILE
ATT_ROWS = 1024
VMEM_LIMIT = 56 * 1024 * 1024

MAX_DIL = max(d for _, d in DIL_PAIRS)
CHUNK = MAX_DIL * BLOCK
TILES_PER_CHUNK = CHUNK // ROW_TILE
STEPS_PER_CHUNK = TILES_PER_CHUNK // STEP_TILES
TILE_RUN = ROW_TILE // MAX_DIL
A12_WIDTH = 2 * A_OUT

F32 = jnp.float32
BF16 = jnp.bfloat16
NT_DIMS = (((1,), (1,)), ((), ()))


def _rms(x, g):
    return x * lax.rsqrt(jnp.mean(x * x, axis=-1, keepdims=True) + EPS) * g


def _const_spec(shape):
    nd = len(shape)
    return pl.BlockSpec(shape, lambda *_: (0,) * nd, pipeline_mode=pl.Buffered(1))


def _params(n_axes=1):
    return pltpu.CompilerParams(
        dimension_semantics=("arbitrary",) * n_axes, vmem_limit_bytes=VMEM_LIMIT)


def _mem_kv_kernel(mem_ref, g_ref, w_ref, mk_ref, mv_ref):
    mn = _rms(mem_ref[0], g_ref[...]).astype(BF16)
    kv = jnp.dot(mn, w_ref[...], preferred_element_type=F32)
    mk_ref[0] = kv[:, :M_WIDTH].astype(BF16)
    mv_ref[0] = kv[:, M_WIDTH:].astype(BF16)


def _mem_kv(mem, g, w):
    b, n, _ = mem.shape
    out = jax.ShapeDtypeStruct((b, n, M_WIDTH), BF16)
    return pl.pallas_call(
        _mem_kv_kernel,
        out_shape=(out, out),
        grid=(b,),
        in_specs=[pl.BlockSpec((1, n, D_MODEL), lambda i: (i, 0, 0)),
                  _const_spec((1, D_MODEL)),
                  _const_spec((D_MODEL, 2 * M_WIDTH))],
        out_specs=(pl.BlockSpec((1, n, M_WIDTH), lambda i: (i, 0, 0)),
                   pl.BlockSpec((1, n, M_WIDTH), lambda i: (i, 0, 0))),
        compiler_params=_params(),
        name="mem_kv",
    )(mem, g, w)


def _ffn_kernel(x_ref, gpre_ref, win_ref, wout_ref, gpost_ref, *rest, emit_u):
    if emit_u:
        gnext_ref, h_ref, u_ref, xn_ref, act_ref, f_ref = rest
    else:
        h_ref, xn_ref, act_ref, f_ref = rest
    rows = lambda tile: slice(tile * ROW_TILE, (tile + 1) * ROW_TILE)

    def prologue(tile):
        xn_ref[tile] = _rms(x_ref[rows(tile), :], gpre_ref[...]).astype(BF16)

    def up_chunk(tile, c):
        lo = c * MXU_N
        xn = xn_ref[tile]
        gate = jnp.dot(xn, win_ref[:, lo:lo + MXU_N], preferred_element_type=F32)
        up = jnp.dot(xn, win_ref[:, D_FF + lo:D_FF + lo + MXU_N], preferred_element_type=F32)
        silu = gate * (1.0 / (1.0 + jnp.exp(-gate)))
        act_ref[tile, :, lo:lo + MXU_N] = (silu * up).astype(BF16)

    def epilogue(tile):
        h = x_ref[rows(tile), :] + 0.5 * _rms(f_ref[tile], gpost_ref[...])
        h_ref[rows(tile), :] = h
        if emit_u:
            u_ref[rows(tile), :] = _rms(h, gnext_ref[...]).astype(BF16)

    n_chunks = D_FF // MXU_N
    prologue(0)
    for c in range(n_chunks):
        up_chunk(0, c)
    for tile in range(STEP_TILES):
        if tile + 1 < STEP_TILES:
            prologue(tile + 1)
            up_chunk(tile + 1, 0)
        f_ref[tile] = jnp.dot(act_ref[tile], wout_ref[...], preferred_element_type=F32)
        epilogue(tile)
        if tile + 1 < STEP_TILES:
            for c in range(1, n_chunks):
                up_chunk(tile + 1, c)


def _ffn(x, gpre, w_in, w_out, gpost, gnext=None):
    t = x.shape[0]
    emit_u = gnext is not None
    row = pl.BlockSpec((STEP_ROWS, D_MODEL), lambda i: (i, 0))
    gain = _const_spec((1, D_MODEL))
    in_specs = [row, gain, _const_spec((D_MODEL, 2 * D_FF)), _const_spec((D_FF, D_MODEL)), gain]
    args = [x, gpre, w_in, w_out, gpost]
    out_shape = [jax.ShapeDtypeStruct((t, D_MODEL), F32)]
    out_specs = [row]
    if emit_u:
        in_specs.append(gain)
        args.append(gnext)
        out_shape.append(jax.ShapeDtypeStruct((t, D_MODEL), BF16))
        out_specs.append(row)
    return pl.pallas_call(
        functools.partial(_ffn_kernel, emit_u=emit_u),
        out_shape=tuple(out_shape),
        grid=(t // STEP_ROWS,),
        in_specs=in_specs,
        out_specs=tuple(out_specs),
        scratch_shapes=[pltpu.VMEM((STEP_TILES, ROW_TILE, D_MODEL), BF16),
                        pltpu.VMEM((STEP_TILES, ROW_TILE, D_FF), BF16),
                        pltpu.VMEM((STEP_TILES, ROW_TILE, D_MODEL), F32)],
        compiler_params=_params(),
        name="ffn_u" if emit_u else "ffn",
    )(*args)


def _proj_kernel(u_ref, w_ref, perm_ref, cq_ref, sq_ref, ck_ref, sk_ref, cqp_ref, sqp_ref, ckp_ref,
                 skp_ref, mk_ref, mv_ref,
                 q0_ref, k0_ref, v0_ref, q12_ref, k12_ref, v12_ref, bq_ref, bk_ref, bv_ref, om_ref):
    for tile in range(STEP_TILES):
        _proj_tile(tile, u_ref, w_ref, perm_ref, cq_ref, sq_ref, ck_ref, sk_ref, cqp_ref, sqp_ref,
                   ckp_ref, skp_ref, mk_ref, mv_ref,
                   q0_ref, k0_ref, v0_ref, q12_ref, k12_ref, v12_ref, bq_ref, bk_ref, bv_ref, om_ref)


def _proj_tile(tile, u_ref, w_ref, perm_ref, cq_ref, sq_ref, ck_ref, sk_ref, cqp_ref, sqp_ref, ckp_ref,
               skp_ref, mk_ref, mv_ref,
               q0_ref, k0_ref, v0_ref, q12_ref, k12_ref, v12_ref, bq_ref, bk_ref, bv_ref, om_ref):
    rows = slice(tile * ROW_TILE, (tile + 1) * ROW_TILE)
    u = u_ref[rows, :]
    u_res = jnp.dot(perm_ref[...], u, preferred_element_type=F32).astype(BF16)

    def proj(lhs, lo):
        return jnp.dot(lhs, w_ref[:, lo:lo + MXU_N], preferred_element_type=F32)

    def rope(y, cos, sin):
        out = []
        for hh in range(MXU_N // HEAD_DIM):
            yh = y[:, hh * HEAD_DIM:(hh + 1) * HEAD_DIM]
            out.append(yh * cos + pltpu.roll(yh, HEAD_DIM // 2, 1) * sin)
        return jnp.concatenate(out, axis=1)

    def put_res(dst, col, y):
        for c in range(MAX_DIL):
            dst[c, tile, :, col:col + MXU_N] = y[c * TILE_RUN:(c + 1) * TILE_RUN].astype(dst.dtype)

    cqp, sqp, ckp, skp = cqp_ref[rows, :], sqp_ref[rows, :], ckp_ref[rows, :], skp_ref[rows, :]
    for g in range(N_GROUPS):
        dst_q, dst_k, dst_v = (q0_ref, k0_ref, v0_ref) if g == 0 else (q12_ref, k12_ref, v12_ref)
        col = 0 if g == 0 else (g - 1) * A_OUT
        put_res(dst_q, col, rope(proj(u_res, OFF_AQ + g * A_OUT), cqp, sqp))
        put_res(dst_k, col, rope(proj(u_res, OFF_AK + g * A_OUT), ckp, skp))
        put_res(dst_v, col, proj(u_res, OFF_AV + g * A_OUT))

    cq, sq, ck, sk = cq_ref[rows, :], sq_ref[rows, :], ck_ref[rows, :], sk_ref[rows, :]
    for c in range(B_WIDTH // MXU_N):
        bq_ref[rows, c * MXU_N:(c + 1) * MXU_N] = rope(proj(u, OFF_BQ + c * MXU_N), cq, sq).astype(BF16)
    bk_ref[rows, :] = rope(proj(u, OFF_BK), ck, sk).astype(BF16)
    bv_ref[rows, :] = proj(u, OFF_BV).astype(BF16)

    for c in range(M_WIDTH // MXU_N):
        y = proj(u, OFF_MQ + c * MXU_N) * QK_SCALE
        for hh in range(MXU_N // HEAD_DIM):
            h = c * (MXU_N // HEAD_DIM) + hh
            cols = slice(h * HEAD_DIM, (h + 1) * HEAD_DIM)
            q = y[:, hh * HEAD_DIM:(hh + 1) * HEAD_DIM].astype(BF16)
            s = lax.dot_general(q, mk_ref[0, :, cols], NT_DIMS, preferred_element_type=F32)
            m = jnp.max(s, axis=-1, keepdims=True)
            p = jnp.exp(s - m)
            den = jnp.sum(p, axis=-1, keepdims=True)
            o = jnp.dot(p.astype(BF16), mv_ref[0, :, cols], preferred_element_type=F32)
            om_ref[rows, cols] = (o / den).astype(BF16)


def _res_spec(width):
    return pl.BlockSpec((None, MAX_DIL, STEP_TILES, TILE_RUN, width),
                        lambda i: (i // STEPS_PER_CHUNK, 0, i % STEPS_PER_CHUNK, 0, 0))


def _proj(u, w_in, perm, tables, tables_res, mk, mv, seq):
    t = u.shape[0]
    n_mem = mk.shape[1]
    tiles_per_seq = seq // STEP_ROWS

    def row(width):
        return pl.BlockSpec((STEP_ROWS, width), lambda i: (i, 0))

    def res_shape(width, dtype):
        return jax.ShapeDtypeStruct((t // CHUNK, MAX_DIL, TILES_PER_CHUNK, TILE_RUN, width), dtype)

    table = pl.BlockSpec((STEP_ROWS, HEAD_DIM), lambda i: (i % tiles_per_seq, 0))
    memb = pl.BlockSpec((1, n_mem, M_WIDTH), lambda i: (i // tiles_per_seq, 0, 0))
    out_shape = ([res_shape(A_OUT, F32)] * 3 + [res_shape(A12_WIDTH, BF16)] * 3
                 + [jax.ShapeDtypeStruct((t, w), BF16) for w in (B_WIDTH, B_KV_WIDTH, B_KV_WIDTH, M_WIDTH)])
    out_specs = ([_res_spec(A_OUT)] * 3 + [_res_spec(A12_WIDTH)] * 3
                 + [row(w) for w in (B_WIDTH, B_KV_WIDTH, B_KV_WIDTH, M_WIDTH)])
    return pl.pallas_call(
        _proj_kernel,
        out_shape=tuple(out_shape),
        grid=(t // STEP_ROWS,),
        in_specs=[row(D_MODEL), _const_spec((D_MODEL, D_IN)), _const_spec((ROW_TILE, ROW_TILE))]
        + [table] * 8 + [memb, memb],
        out_specs=tuple(out_specs),
        compiler_params=_params(),
        name="proj",
    )(u, w_in, perm, *tables, *tables_res, mk, mv)


def _band_kernel(sink_ref, q_ref, k_ref, v_ref, kp_ref, vp_ref, o_ref, *, hq, hkv, max_dist, n_blk):
    grp = hq // hkv
    first_tile = pl.program_id(1) == 0

    row = lax.broadcasted_iota(jnp.int32, (BLOCK, 2 * BLOCK), 0)
    col = lax.broadcasted_iota(jnp.int32, (BLOCK, 2 * BLOCK), 1)
    dist = row + BLOCK - col
    band = (dist >= 0) & (dist <= max_dist)
    band_first = band & ((col >= BLOCK) | jnp.logical_not(first_tile))

    for j in range(n_blk):
        rows = slice(j * BLOCK, (j + 1) * BLOCK)
        mask = band_first if j == 0 else band
        for hk in range(hkv):
            kcols = slice(hk * HEAD_DIM, (hk + 1) * HEAD_DIM)
            if j == 0:
                kk = jnp.concatenate([kp_ref[0, :, kcols], k_ref[0, rows, kcols]], axis=0)
                vv = jnp.concatenate([vp_ref[0, :, kcols], v_ref[0, rows, kcols]], axis=0)
            else:
                both = slice((j - 1) * BLOCK, (j + 1) * BLOCK)
                kk = k_ref[0, both, kcols]
                vv = v_ref[0, both, kcols]
            for g in range(grp):
                h = hk * grp + g
                qcols = slice(h * HEAD_DIM, (h + 1) * HEAD_DIM)
                s = lax.dot_general(q_ref[0, rows, qcols], kk, NT_DIMS, preferred_element_type=F32)
                s = jnp.where(mask, s, NEG_INF)
                sk = sink_ref[h]
                m = jnp.maximum(jnp.max(s, axis=-1, keepdims=True), sk)
                p = jnp.exp(s - m)
                tot = jnp.sum(p, axis=-1, keepdims=True) + jnp.exp(sk - m)
                o = jnp.dot(p.astype(BF16), vv, preferred_element_type=F32)
                o_ref[0, rows, qcols] = (o / tot).astype(BF16)


def _band_attention(q, k, v, max_dist, sink):
    n, length, qw = q.shape
    kw = k.shape[2]
    rows = min(ATT_ROWS, length)
    n_blk = rows // BLOCK
    cur = lambda w: pl.BlockSpec((1, rows, w), lambda b, i: (b, i, 0))
    prev = pl.BlockSpec((1, BLOCK, kw), lambda b, i: (b, jnp.maximum(i * n_blk - 1, 0), 0))
    return pl.pallas_call(
        functools.partial(_band_kernel, hq=qw // HEAD_DIM, hkv=kw // HEAD_DIM, max_dist=max_dist,
                          n_blk=n_blk),
        out_shape=jax.ShapeDtypeStruct((n, length, qw), BF16),
        grid=(n, length // rows),
        in_specs=[pl.BlockSpec(memory_space=pltpu.SMEM), cur(qw), cur(kw), cur(kw), prev, prev],
        out_specs=cur(qw),
        compiler_params=_params(2),
        name="band_sink",
    )(sink, q, k, v, k, v)


def _dilated_kernel(q0_ref, k0_ref, v0_ref, k0p_ref, v0p_ref,
                    q12_ref, k12_ref, v12_ref, k1p_ref, v1p_ref, k2p_ref, v2p_ref,
                    o_ref, og_ref, lg_ref):
    first_chunk = pl.program_id(1) == 0
    row = lax.broadcasted_iota(jnp.int32, (BLOCK, 2 * BLOCK), 0)
    col = lax.broadcasted_iota(jnp.int32, (BLOCK, 2 * BLOCK), 1)
    own = col >= BLOCK
    kcol = jnp.where(own, col - BLOCK, col)

    def masks(n_pieces):
        piece = BLOCK // n_pieces
        shift = piece.bit_length() - 1
        pos = lambda r: (r & (piece - 1)) * n_pieces + (r >> shift)
        dist = pos(row) - pos(kcol) + jnp.where(own, 0, BLOCK)
        band = (dist >= 0) & (dist <= BLOCK)
        return band, band & (own | jnp.logical_not(first_chunk))

    def attend(q, kk, vv, mask):
        s = lax.dot_general(q, kk, NT_DIMS, preferred_element_type=F32)
        s = jnp.where(mask, s, NEG_INF)
        m = jnp.max(s, axis=-1, keepdims=True)
        p = jnp.exp(s - m)
        den = jnp.sum(p, axis=-1, keepdims=True)
        o = jnp.dot(p.astype(BF16), vv, preferred_element_type=F32) / den
        return o, jnp.broadcast_to(m + jnp.log(den), (BLOCK, HEAD_DIM))

    def gather(ref, starts, piece, cols):
        return jnp.concatenate([ref[s:s + piece, cols] for s in starts], axis=0).astype(BF16)

    for g, n_pieces in ((0, MAX_DIL), (1, MAX_DIL // DIL_PAIRS[1][1])):
        piece = BLOCK // n_pieces
        band, band_first = masks(n_pieces)
        q_ref, k_ref, v_ref, kp_ref, vp_ref = ((q0_ref, k0_ref, v0_ref, k0p_ref, v0p_ref) if g == 0 else
                                               (q12_ref, k12_ref, v12_ref, k1p_ref, v1p_ref))
        n_res = MAX_DIL // n_pieces
        for res in range(n_res):
            for h in range(A_HEADS_PER_GROUP):
                cols = slice(h * HEAD_DIM, (h + 1) * HEAD_DIM)
                bases = [(res + n_res * p) * BLOCK for p in range(n_pieces)]
                k_prev = jnp.concatenate([kp_ref[res + n_res * p, :, cols] for p in range(n_pieces)],
                                         axis=0).astype(BF16)
                v_prev = jnp.concatenate([vp_ref[res + n_res * p, :, cols] for p in range(n_pieces)],
                                         axis=0).astype(BF16)
                for a in range(BLOCK // piece):
                    starts = [b0 + a * piece for b0 in bases]
                    k_own = gather(k_ref, starts, piece, cols)
                    v_own = gather(v_ref, starts, piece, cols)
                    o, lse = attend(gather(q_ref, starts, piece, cols),
                                    jnp.concatenate([k_prev, k_own], axis=0),
                                    jnp.concatenate([v_prev, v_own], axis=0),
                                    band_first if a == 0 else band)
                    for p, s0 in enumerate(starts):
                        og_ref[g, s0:s0 + piece, cols] = o[p * piece:(p + 1) * piece]
                        lg_ref[g, s0:s0 + piece, cols] = lse[p * piece:(p + 1) * piece]
                    k_prev, v_prev = k_own, v_own

    band, band_first = masks(1)
    del band
    for c in range(MAX_DIL):
        rows = slice(c * BLOCK, (c + 1) * BLOCK)
        for h in range(A_HEADS_PER_GROUP):
            cols = slice(h * HEAD_DIM, (h + 1) * HEAD_DIM)
            cols2 = slice(A_OUT + h * HEAD_DIM, A_OUT + (h + 1) * HEAD_DIM)
            o2, l2 = attend(q12_ref[rows, cols2],
                            jnp.concatenate([k2p_ref[rows, cols], k12_ref[rows, cols2]], axis=0),
                            jnp.concatenate([v2p_ref[rows, cols], v12_ref[rows, cols2]], axis=0),
                            band_first)
            l0, l1 = lg_ref[0, rows, cols], lg_ref[1, rows, cols]
            mx = jnp.maximum(jnp.maximum(l0, l1), l2)
            e0, e1, e2 = jnp.exp(l0 - mx), jnp.exp(l1 - mx), jnp.exp(l2 - mx)
            den = e0 + e1 + e2
            acc = (e0 / den) * og_ref[0, rows, cols] + (e1 / den) * og_ref[1, rows, cols] + (e2 / den) * o2
            o_ref[rows, cols] = acc.astype(BF16)


def _dilated_attention(q0, k0, v0, q12, k12, v12, batch):
    t = q0.shape[0]
    n_chunks = t // CHUNK
    per_seq = n_chunks // batch
    chunk = lambda b, i: b * per_seq + i
    prev = lambda b, i: b * per_seq + jnp.maximum(i - 1, 0)

    def cur(width):
        return pl.BlockSpec((CHUNK, width), lambda b, i: (chunk(b, i), 0))

    def tail(piece, width, col):
        return pl.BlockSpec((None, MAX_DIL, None, piece, width),
                            lambda b, i: (prev(b, i), 0, BLOCK // piece - 1, 0, col))

    def tail_view(x, piece):
        return x.reshape(n_chunks, MAX_DIL, BLOCK // piece, piece, x.shape[-1])

    piece0 = BLOCK // MAX_DIL
    piece1 = BLOCK // (MAX_DIL // DIL_PAIRS[1][1])
    prev2 = pl.BlockSpec((None, CHUNK, A_OUT), lambda b, i: (prev(b, i), 0, 1))
    view2 = lambda x: x.reshape(n_chunks, CHUNK, A12_WIDTH)
    return pl.pallas_call(
        _dilated_kernel,
        out_shape=jax.ShapeDtypeStruct((t, A_OUT), BF16),
        grid=(batch, per_seq),
        in_specs=[cur(A_OUT), cur(A_OUT), cur(A_OUT), tail(piece0, A_OUT, 0), tail(piece0, A_OUT, 0),
                  cur(A12_WIDTH), cur(A12_WIDTH), cur(A12_WIDTH),
                  tail(piece1, A_OUT, 0), tail(piece1, A_OUT, 0), prev2, prev2],
        out_specs=cur(A_OUT),
        scratch_shapes=[pltpu.VMEM((2, CHUNK, A_OUT), F32), pltpu.VMEM((2, CHUNK, A_OUT), F32)],
        compiler_params=_params(2),
        name="dilated",
    )(q0, k0, v0, tail_view(k0, piece0), tail_view(v0, piece0),
      q12, k12, v12, tail_view(k12, piece1), tail_view(v12, piece1), view2(k12), view2(v12))


def _merge_kernel(h_ref, u_ref, oa_ref, ob_ref, om_ref, unperm_ref,
                  wg_ref, bg_ref, woa_ref, wob_ref, wom_ref, wout_ref, gpost_ref, out_ref):
    for tile in range(STEP_TILES):
        rows = slice(tile * ROW_TILE, (tile + 1) * ROW_TILE)
        u = u_ref[rows, :]

        def gate(idx):
            cols = slice(idx * D_MODEL, (idx + 1) * D_MODEL)
            z = jnp.dot(u, wg_ref[:, cols], preferred_element_type=F32) + bg_ref[:, cols]
            return 1.0 / (1.0 + jnp.exp(-z))

        oa_res = jnp.concatenate([oa_ref[c, tile] for c in range(MAX_DIL)], axis=0)
        o_a = jnp.dot(unperm_ref[...], oa_res, preferred_element_type=F32).astype(BF16)

        merged = gate(0) * jnp.dot(o_a, woa_ref[...], preferred_element_type=F32)
        merged = merged + gate(1) * jnp.dot(ob_ref[rows, :], wob_ref[...], preferred_element_type=F32)
        merged = merged + gate(2) * jnp.dot(om_ref[rows, :], wom_ref[...], preferred_element_type=F32)
        mixed = jnp.dot(merged.astype(BF16), wout_ref[...], preferred_element_type=F32)
        out_ref[rows, :] = h_ref[rows, :] + _rms(mixed, gpost_ref[...])


def _merge(h, u, oa, ob, om, unperm, wg, bg, woa, wob, wom, wout, gpost):
    t = h.shape[0]

    def row(width):
        return pl.BlockSpec((STEP_ROWS, width), lambda i: (i, 0))

    in_specs = ([row(D_MODEL), row(D_MODEL), _res_spec(A_OUT), row(B_WIDTH), row(M_WIDTH),
                 _const_spec((ROW_TILE, ROW_TILE)),
                 _const_spec((D_MODEL, 3 * D_MODEL)), _const_spec((1, 3 * D_MODEL)),
                 _const_spec((A_OUT, D_MODEL)), _const_spec((B_WIDTH, D_MODEL)),
                 _const_spec((M_WIDTH, D_MODEL)), _const_spec((D_MODEL, D_MODEL)),
                 _const_spec((1, D_MODEL))])
    oa = oa.reshape(t // CHUNK, MAX_DIL, TILES_PER_CHUNK, TILE_RUN, A_OUT)
    return pl.pallas_call(
        _merge_kernel,
        out_shape=jax.ShapeDtypeStruct((t, D_MODEL), F32),
        grid=(t // STEP_ROWS,),
        in_specs=in_specs,
        out_specs=row(D_MODEL),
        compiler_params=_params(),
        name="merge",
    )(h, u, oa, ob, om, unperm, wg, bg, woa, wob, wom, wout, gpost)


def _rope_tables(seq):
    half = HEAD_DIM // 2
    inv = ROPE_THETA ** (-jnp.arange(half, dtype=F32) / half)
    ang = jnp.arange(seq).astype(F32)[:, None] * inv[None, :]
    cos, sin = jnp.cos(ang), jnp.sin(ang)
    cos = jnp.concatenate([cos, cos], axis=-1)
    sin = jnp.concatenate([-sin, sin], axis=-1)
    return cos * QK_SCALE, sin * QK_SCALE, cos, sin


def _tile_residue_major(table):
    s, w = table.shape
    return (table.reshape(s // ROW_TILE, TILE_RUN, MAX_DIL, w).transpose(0, 2, 1, 3).reshape(s, w))


def _residue_perm():
    out_row = np.arange(ROW_TILE)
    src = (out_row % TILE_RUN) * MAX_DIL + out_row // TILE_RUN
    return np.equal(src[:, None], np.arange(ROW_TILE)[None, :]).astype(np.float32)


def kernel(x, mem, ffn1_norm_pre, ffn1_w_in, ffn1_w_out, ffn1_norm_post, mix_norm_pre, w_in, sinks, mem_norm, w_mem_kv, w_gate, b_gate, w_o_a, w_o_b, w_o_m, w_out, mix_norm_post, ffn2_norm_pre, ffn2_w_in, ffn2_w_out, ffn2_norm_post):
    b, s, _ = x.shape
    depth = ffn1_w_in.shape[0]
    t = b * s
    tables = _rope_tables(s)
    tables_res = tuple(_tile_residue_major(tb) for tb in tables)
    perm = jnp.asarray(_residue_perm(), dtype=BF16)
    unperm = jnp.asarray(_residue_perm().T, dtype=BF16)
    h = x.reshape(t, D_MODEL)
    for l in range(depth):
        bf = lambda w: w[l].astype(BF16)
        mk, mv = _mem_kv(mem, mem_norm[l][None], bf(w_mem_kv))
        h1, u = _ffn(h, ffn1_norm_pre[l][None], bf(ffn1_w_in), bf(ffn1_w_out),
                     ffn1_norm_post[l][None], mix_norm_pre[l][None])
        q0, k0, v0, q12, k12, v12, bq, bk, bv, om = _proj(u, bf(w_in), perm, tables, tables_res, mk, mv, s)

        flat = lambda a: a.reshape(t, a.shape[-1])
        oa = _dilated_attention(flat(q0), flat(k0), flat(v0), flat(q12), flat(k12), flat(v12), b)
        seq3 = lambda a: a.reshape(b, s, a.shape[-1])
        ob = _band_attention(seq3(bq), seq3(bk), seq3(bv), B_WINDOW - 1, sinks[l]).reshape(t, B_WIDTH)

        h2 = _merge(h1, u, oa, ob, om, unperm, bf(w_gate), b_gate[l][None], bf(w_o_a), bf(w_o_b),
                    bf(w_o_m), bf(w_out), mix_norm_post[l][None])
        (h,) = _ffn(h2, ffn2_norm_pre[l][None], bf(ffn2_w_in), bf(ffn2_w_out), ffn2_norm_post[l][None])
    return h.reshape(b, s, D_MODEL)
```

```python
import functools

import jax
import jax.numpy as jnp
import numpy as np
from jax import lax
from jax.experimental import pallas as pl
from jax.experimental.pallas import tpu as pltpu

D_MODEL = 1024
HEAD_DIM = 128
DIL_PAIRS = ((128, 1), (512, 4), (2048, 16))
A_HEADS_PER_GROUP = 2
N_GROUPS = len(DIL_PAIRS)
A_HEADS = A_HEADS_PER_GROUP * N_GROUPS
B_Q_HEADS = 4
B_KV_HEADS = 2
B_WINDOW = 128
M_HEADS = 4
D_FF = 2816
ROPE_THETA = 10000.0
BLOCK = 128
EPS = 1e-6
NEG_INF = -1e30

A_WIDTH = A_HEADS * HEAD_DIM
A_OUT = A_HEADS_PER_GROUP * HEAD_DIM
B_WIDTH = B_Q_HEADS * HEAD_DIM
B_KV_WIDTH = B_KV_HEADS * HEAD_DIM
M_WIDTH = M_HEADS * HEAD_DIM
D_IN = 3 * A_WIDTH + B_WIDTH + 2 * B_KV_WIDTH + M_WIDTH
OFF_AQ = 0
OFF_AK = A_WIDTH
OFF_AV = 2 * A_WIDTH
OFF_BQ = 3 * A_WIDTH
OFF_BK = OFF_BQ + B_WIDTH
OFF_BV = OFF_BK + B_KV_WIDTH
OFF_MQ = OFF_BV + B_KV_WIDTH
QK_SCALE = HEAD_DIM ** -0.5

MXU_N = 256
ROW_TILE = 512
STEP_TILES = 2
STEP_ROWS = STEP_TILES * ROW_TILE
ATT_ROWS = 1024
VMEM_LIMIT = 56 * 1024 * 1024

MAX_DIL = max(d for _, d in DIL_PAIRS)
CHUNK = MAX_DIL * BLOCK
TILES_PER_CHUNK = CHUNK // ROW_TILE
STEPS_PER_CHUNK = TILES_PER_CHUNK // STEP_TILES
TILE_RUN = ROW_TILE // MAX_DIL
A12_WIDTH = 2 * A_OUT

F32 = jnp.float32
BF16 = jnp.bfloat16
NT_DIMS = (((1,), (1,)), ((), ()))


def _rms(x, g):
    return x * lax.rsqrt(jnp.mean(x * x, axis=-1, keepdims=True) + EPS) * g


def _const_spec(shape):
    nd = len(shape)
    return pl.BlockSpec(shape, lambda *_: (0,) * nd, pipeline_mode=pl.Buffered(1))


def _params(n_axes=1):
    return pltpu.CompilerParams(
        dimension_semantics=("arbitrary",) * n_axes, vmem_limit_bytes=VMEM_LIMIT)


def _mem_kv_kernel(mem_ref, g_ref, w_ref, mk_ref, mv_ref):
    mn = _rms(mem_ref[0], g_ref[...]).astype(BF16)
    kv = jnp.dot(mn, w_ref[...], preferred_element_type=F32)
    mk_ref[0] = kv[:, :M_WIDTH].astype(BF16)
    mv_ref[0] = kv[:, M_WIDTH:].astype(BF16)


def _mem_kv(mem, g, w):
    b, n, _ = mem.shape
    out = jax.ShapeDtypeStruct((b, n, M_WIDTH), BF16)
    return pl.pallas_call(
        _mem_kv_kernel,
        out_shape=(out, out),
        grid=(b,),
        in_specs=[pl.BlockSpec((1, n, D_MODEL), lambda i: (i, 0, 0)),
                  _const_spec((1, D_MODEL)),
                  _const_spec((D_MODEL, 2 * M_WIDTH))],
        out_specs=(pl.BlockSpec((1, n, M_WIDTH), lambda i: (i, 0, 0)),
                   pl.BlockSpec((1, n, M_WIDTH), lambda i: (i, 0, 0))),
        compiler_params=_params(),
        name="mem_kv",
    )(mem, g, w)


def _ffn_kernel(x_ref, gpre_ref, win_ref, wout_ref, gpost_ref, *rest, emit_u):
    if emit_u:
        gnext_ref, h_ref, u_ref, xn_ref, act_ref, f_ref = rest
    else:
        h_ref, xn_ref, act_ref, f_ref = rest
    rows = lambda tile: slice(tile * ROW_TILE, (tile + 1) * ROW_TILE)

    def prologue(tile):
        xn_ref[tile] = _rms(x_ref[rows(tile), :], gpre_ref[...]).astype(BF16)

    def up_chunk(tile, c):
        lo = c * MXU_N
        xn = xn_ref[tile]
        gate = jnp.dot(xn, win_ref[:, lo:lo + MXU_N], preferred_element_type=F32)
        up = jnp.dot(xn, win_ref[:, D_FF + lo:D_FF + lo + MXU_N], preferred_element_type=F32)
        silu = gate * (1.0 / (1.0 + jnp.exp(-gate)))
        act_ref[tile, :, lo:lo + MXU_N] = (silu * up).astype(BF16)

    def epilogue(tile):
        h = x_ref[rows(tile), :] + 0.5 * _rms(f_ref[tile], gpost_ref[...])
        h_ref[rows(tile), :] = h
        if emit_u:
            u_ref[rows(tile), :] = _rms(h, gnext_ref[...]).astype(BF16)

    n_chunks = D_FF // MXU_N
    prologue(0)
    for c in range(n_chunks):
        up_chunk(0, c)
    for tile in range(STEP_TILES):
        if tile + 1 < STEP_TILES:
            prologue(tile + 1)
            up_chunk(tile + 1, 0)
        f_ref[tile] = jnp.dot(act_ref[tile], wout_ref[...], preferred_element_type=F32)
        epilogue(tile)
        if tile + 1 < STEP_TILES:
            for c in range(1, n_chunks):
                up_chunk(tile + 1, c)


def _ffn(x, gpre, w_in, w_out, gpost, gnext=None):
    t = x.shape[0]
    emit_u = gnext is not None
    row = pl.BlockSpec((STEP_ROWS, D_MODEL), lambda i: (i, 0))
    gain = _const_spec((1, D_MODEL))
    in_specs = [row, gain, _const_spec((D_MODEL, 2 * D_FF)), _const_spec((D_FF, D_MODEL)), gain]
    args = [x, gpre, w_in, w_out, gpost]
    out_shape = [jax.ShapeDtypeStruct((t, D_MODEL), F32)]
    out_specs = [row]
    if emit_u:
        in_specs.append(gain)
        args.append(gnext)
        out_shape.append(jax.ShapeDtypeStruct((t, D_MODEL), BF16))
        out_specs.append(row)
    return pl.pallas_call(
        functools.partial(_ffn_kernel, emit_u=emit_u),
        out_shape=tuple(out_shape),
        grid=(t // STEP_ROWS,),
        in_specs=in_specs,
        out_specs=tuple(out_specs),
        scratch_shapes=[pltpu.VMEM((STEP_TILES, ROW_TILE, D_MODEL), BF16),
                        pltpu.VMEM((STEP_TILES, ROW_TILE, D_FF), BF16),
                        pltpu.VMEM((STEP_TILES, ROW_TILE, D_MODEL), F32)],
        compiler_params=_params(),
        name="ffn_u" if emit_u else "ffn",
    )(*args)


def _proj_kernel(u_ref, w_ref, perm_ref, cb_ref, sb_ref, cl_ref, sl_ref, clr_ref, slr_ref, mk_ref, mv_ref,
                 q0_ref, k0_ref, v0_ref, q12_ref, k12_ref, v12_ref, bq_ref, bk_ref, bv_ref, om_ref,
                 *, steps_per_seq):
    step = pl.program_id(0) % steps_per_seq
    cb, sb = cb_ref[pl.ds(step, 1), :], sb_ref[pl.ds(step, 1), :]
    lane = lax.broadcasted_iota(jnp.int32, (1, HEAD_DIM), 1)
    sign = jnp.where(lane < HEAD_DIM // 2, -1.0, 1.0)

    def rope_tables(cl, sl, scale):
        cos = (cb * scale) * cl - (sb * scale) * sl
        sin = (sb * (sign * scale)) * cl + (cb * (sign * scale)) * sl
        return cos, sin

    for tile in range(STEP_TILES):
        rows = slice(tile * ROW_TILE, (tile + 1) * ROW_TILE)
        cl, sl, clr, slr = cl_ref[rows, :], sl_ref[rows, :], clr_ref[rows, :], slr_ref[rows, :]
        _proj_tile(tile, u_ref, w_ref, perm_ref,
                   rope_tables(cl, sl, QK_SCALE) + rope_tables(cl, sl, 1.0),
                   rope_tables(clr, slr, QK_SCALE) + rope_tables(clr, slr, 1.0), mk_ref, mv_ref,
                   q0_ref, k0_ref, v0_ref, q12_ref, k12_ref, v12_ref, bq_ref, bk_ref, bv_ref, om_ref)


def _proj_tile(tile, u_ref, w_ref, perm_ref, tables, tables_res, mk_ref, mv_ref,
               q0_ref, k0_ref, v0_ref, q12_ref, k12_ref, v12_ref, bq_ref, bk_ref, bv_ref, om_ref):
    rows = slice(tile * ROW_TILE, (tile + 1) * ROW_TILE)
    u = u_ref[rows, :]
    u_res = jnp.dot(perm_ref[...], u, preferred_element_type=F32).astype(BF16)

    def proj(lhs, lo):
        return jnp.dot(lhs, w_ref[:, lo:lo + MXU_N], preferred_element_type=F32)

    def rope(y, cos, sin):
        out = []
        for hh in range(MXU_N // HEAD_DIM):
            yh = y[:, hh * HEAD_DIM:(hh + 1) * HEAD_DIM]
            out.append(yh * cos + pltpu.roll(yh, HEAD_DIM // 2, 1) * sin)
        return jnp.concatenate(out, axis=1)

    def put_res(dst, col, y):
        for c in range(MAX_DIL):
            dst[c, tile, :, col:col + MXU_N] = y[c * TILE_RUN:(c + 1) * TILE_RUN].astype(dst.dtype)

    cqp, sqp, ckp, skp = tables_res
    for g in range(N_GROUPS):
        dst_q, dst_k, dst_v = (q0_ref, k0_ref, v0_ref) if g == 0 else (q12_ref, k12_ref, v12_ref)
        col = 0 if g == 0 else (g - 1) * A_OUT
        put_res(dst_q, col, rope(proj(u_res, OFF_AQ + g * A_OUT), cqp, sqp))
        put_res(dst_k, col, rope(proj(u_res, OFF_AK + g * A_OUT), ckp, skp))
        put_res(dst_v, col, proj(u_res, OFF_AV + g * A_OUT))

    cq, sq, ck, sk = tables
    for c in range(B_WIDTH // MXU_N):
        bq_ref[rows, c * MXU_N:(c + 1) * MXU_N] = rope(proj(u, OFF_BQ + c * MXU_N), cq, sq).astype(BF16)
    bk_ref[rows, :] = rope(proj(u, OFF_BK), ck, sk).astype(BF16)
    bv_ref[rows, :] = proj(u, OFF_BV).astype(BF16)

    for c in range(M_WIDTH // MXU_N):
        y = proj(u, OFF_MQ + c * MXU_N) * QK_SCALE
        for hh in range(MXU_N // HEAD_DIM):
            h = c * (MXU_N // HEAD_DIM) + hh
            cols = slice(h * HEAD_DIM, (h + 1) * HEAD_DIM)
            q = y[:, hh * HEAD_DIM:(hh + 1) * HEAD_DIM].astype(BF16)
            s = lax.dot_general(q, mk_ref[0, :, cols], NT_DIMS, preferred_element_type=F32)
            m = jnp.max(s, axis=-1, keepdims=True)
            p = jnp.exp(s - m)
            den = jnp.sum(p, axis=-1, keepdims=True)
            o = jnp.dot(p.astype(BF16), mv_ref[0, :, cols], preferred_element_type=F32)
            om_ref[rows, cols] = (o / den).astype(BF16)


def _res_spec(width):
    return pl.BlockSpec((None, MAX_DIL, STEP_TILES, TILE_RUN, width),
                        lambda i: (i // STEPS_PER_CHUNK, 0, i % STEPS_PER_CHUNK, 0, 0))


def _proj(u, w_in, perm, rope_consts, mk, mv, seq):
    t = u.shape[0]
    n_mem = mk.shape[1]
    tiles_per_seq = seq // STEP_ROWS

    def row(width):
        return pl.BlockSpec((STEP_ROWS, width), lambda i: (i, 0))

    def res_shape(width, dtype):
        return jax.ShapeDtypeStruct((t // CHUNK, MAX_DIL, TILES_PER_CHUNK, TILE_RUN, width), dtype)

    base_table = _const_spec((tiles_per_seq, HEAD_DIM))
    local_table = _const_spec((STEP_ROWS, HEAD_DIM))
    memb =pl.BlockSpec((1, n_mem, M_WIDTH), lambda i: (i // tiles_per_seq, 0, 0))
    out_shape = ([res_shape(A_OUT, F32)] * 3 + [res_shape(A12_WIDTH, BF16)] * 3
                 + [jax.ShapeDtypeStruct((t, w), BF16) for w in (B_WIDTH, B_KV_WIDTH, B_KV_WIDTH, M_WIDTH)])
    out_specs = ([_res_spec(A_OUT)] * 3 + [_res_spec(A12_WIDTH)] * 3
                 + [row(w) for w in (B_WIDTH, B_KV_WIDTH, B_KV_WIDTH, M_WIDTH)])
    return pl.pallas_call(
        functools.partial(_proj_kernel, steps_per_seq=tiles_per_seq),
        out_shape=tuple(out_shape),
        grid=(t // STEP_ROWS,),
        in_specs=[row(D_MODEL), _const_spec((D_MODEL, D_IN)), _const_spec((ROW_TILE, ROW_TILE))]
        + [base_table] * 2 + [local_table] * 4 + [memb, memb],
        out_specs=tuple(out_specs),
        compiler_params=_params(),
        name="proj",
    )(u, w_in, perm, *rope_consts, mk, mv)


def _band_kernel(sink_ref, q_ref, k_ref, v_ref, kp_ref, vp_ref, o_ref, *, hq, hkv, max_dist, n_blk):
    grp = hq // hkv
    first_tile = pl.program_id(1) == 0

    row = lax.broadcasted_iota(jnp.int32, (BLOCK, 2 * BLOCK), 0)
    col = lax.broadcasted_iota(jnp.int32, (BLOCK, 2 * BLOCK), 1)
    dist = row + BLOCK - col
    band = (dist >= 0) & (dist <= max_dist)
    band_first = band & ((col >= BLOCK) | jnp.logical_not(first_tile))

    for j in range(n_blk):
        rows = slice(j * BLOCK, (j + 1) * BLOCK)
        mask = band_first if j == 0 else band
        for hk in range(hkv):
            kcols = slice(hk * HEAD_DIM, (hk + 1) * HEAD_DIM)
            if j == 0:
                kk = jnp.concatenate([kp_ref[0, :, kcols], k_ref[0, rows, kcols]], axis=0)
                vv = jnp.concatenate([vp_ref[0, :, kcols], v_ref[0, rows, kcols]], axis=0)
            else:
                both = slice((j - 1) * BLOCK, (j + 1) * BLOCK)
                kk = k_ref[0, both, kcols]
                vv = v_ref[0, both, kcols]
            for g in range(grp):
                h = hk * grp + g
                qcols = slice(h * HEAD_DIM, (h + 1) * HEAD_DIM)
                s = lax.dot_general(q_ref[0, rows, qcols], kk, NT_DIMS, preferred_element_type=F32)
                s = jnp.where(mask, s, NEG_INF)
                sk = sink_ref[h]
                m = jnp.maximum(jnp.max(s, axis=-1, keepdims=True), sk)
                p = jnp.exp(s - m)
                tot = jnp.sum(p, axis=-1, keepdims=True) + jnp.exp(sk - m)
                o = jnp.dot(p.astype(BF16), vv, preferred_element_type=F32)
                o_ref[0, rows, qcols] = (o / tot).astype(BF16)


def _band_attention(q, k, v, max_dist, sink):
    n, length, qw = q.shape
    kw = k.shape[2]
    rows = min(ATT_ROWS, length)
    n_blk = rows // BLOCK
    cur = lambda w: pl.BlockSpec((1, rows, w), lambda b, i: (b, i, 0))
    prev = pl.BlockSpec((1, BLOCK, kw), lambda b, i: (b, jnp.maximum(i * n_blk - 1, 0), 0))
    return pl.pallas_call(
        functools.partial(_band_kernel, hq=qw // HEAD_DIM, hkv=kw // HEAD_DIM, max_dist=max_dist,
                          n_blk=n_blk),
        out_shape=jax.ShapeDtypeStruct((n, length, qw), BF16),
        grid=(n, length // rows),
        in_specs=[pl.BlockSpec(memory_space=pltpu.SMEM), cur(qw), cur(kw), cur(kw), prev, prev],
        out_specs=cur(qw),
        compiler_params=_params(2),
        name="band_sink",
    )(sink, q, k, v, k, v)


def _dilated_kernel(q0_ref, k0_ref, v0_ref, k0p_ref, v0p_ref,
                    q12_ref, k12_ref, v12_ref, k1p_ref, v1p_ref, k2p_ref, v2p_ref,
                    o_ref, og_ref, lg_ref):
    first_chunk = pl.program_id(1) == 0
    row = lax.broadcasted_iota(jnp.int32, (BLOCK, 2 * BLOCK), 0)
    col = lax.broadcasted_iota(jnp.int32, (BLOCK, 2 * BLOCK), 1)
    own = col >= BLOCK
    kcol = jnp.where(own, col - BLOCK, col)

    def masks(n_pieces):
        piece = BLOCK // n_pieces
        shift = piece.bit_length() - 1
        pos = lambda r: (r & (piece - 1)) * n_pieces + (r >> shift)
        dist = pos(row) - pos(kcol) + jnp.where(own, 0, BLOCK)
        band = (dist >= 0) & (dist <= BLOCK)
        return band, band & (own | jnp.logical_not(first_chunk))

    def attend(q, kk, vv, mask):
        s = lax.dot_general(q, kk, NT_DIMS, preferred_element_type=F32)
        s = jnp.where(mask, s, NEG_INF)
        m = jnp.max(s, axis=-1, keepdims=True)
        p = jnp.exp(s - m)
        den = jnp.sum(p, axis=-1, keepdims=True)
        o = jnp.dot(p.astype(BF16), vv, preferred_element_type=F32) / den
        return o, jnp.broadcast_to(m + jnp.log(den), (BLOCK, HEAD_DIM))

    def gather(ref, starts, piece, cols):
        return jnp.concatenate([ref[s:s + piece, cols] for s in starts], axis=0).astype(BF16)

    for g, n_pieces in ((0, MAX_DIL), (1, MAX_DIL // DIL_PAIRS[1][1])):
        piece = BLOCK // n_pieces
        band, band_first = masks(n_pieces)
        q_ref, k_ref, v_ref, kp_ref, vp_ref = ((q0_ref, k0_ref, v0_ref, k0p_ref, v0p_ref) if g == 0 else
                                               (q12_ref, k12_ref, v12_ref, k1p_ref, v1p_ref))
        n_res = MAX_DIL // n_pieces
        for res in range(n_res):
            for h in range(A_HEADS_PER_GROUP):
                cols = slice(h * HEAD_DIM, (h + 1) * HEAD_DIM)
                bases = [(res + n_res * p) * BLOCK for p in range(n_pieces)]
                k_prev = jnp.concatenate([kp_ref[res + n_res * p, :, cols] for p in range(n_pieces)],
                                         axis=0).astype(BF16)
                v_prev = jnp.concatenate([vp_ref[res + n_res * p, :, cols] for p in range(n_pieces)],
                                         axis=0).astype(BF16)
                for a in range(BLOCK // piece):
                    starts = [b0 + a * piece for b0 in bases]
                    k_own = gather(k_ref, starts, piece, cols)
                    v_own = gather(v_ref, starts, piece, cols)
                    o, lse = attend(gather(q_ref, starts, piece, cols),
                                    jnp.concatenate([k_prev, k_own], axis=0),
                                    jnp.concatenate([v_prev, v_own], axis=0),
                                    band_first if a == 0 else band)
                    for p, s0 in enumerate(starts):
                        og_ref[g, s0:s0 + piece, cols] = o[p * piece:(p + 1) * piece]
                        lg_ref[g, s0:s0 + piece, cols] = lse[p * piece:(p + 1) * piece]
                    k_prev, v_prev = k_own, v_own

    band, band_first = masks(1)
    del band
    for c in range(MAX_DIL):
        rows = slice(c * BLOCK, (c + 1) * BLOCK)
        for h in range(A_HEADS_PER_GROUP):
            cols = slice(h * HEAD_DIM, (h + 1) * HEAD_DIM)
            cols2 = slice(A_OUT + h * HEAD_DIM, A_OUT + (h + 1) * HEAD_DIM)
            o2, l2 = attend(q12_ref[rows, cols2],
                            jnp.concatenate([k2p_ref[rows, cols], k12_ref[rows, cols2]], axis=0),
                            jnp.concatenate([v2p_ref[rows, cols], v12_ref[rows, cols2]], axis=0),
                            band_first)
            l0, l1 = lg_ref[0, rows, cols], lg_ref[1, rows, cols]
            mx = jnp.maximum(jnp.maximum(l0, l1), l2)
            e0, e1, e2 = jnp.exp(l0 - mx), jnp.exp(l1 - mx), jnp.exp(l2 - mx)
            den = e0 + e1 + e2
            acc = (e0 / den) * og_ref[0, rows, cols] + (e1 / den) * og_ref[1, rows, cols] + (e2 / den) * o2
            o_ref[rows, cols] = acc.astype(BF16)


def _dilated_attention(q0, k0, v0, q12, k12, v12, batch):
    t = q0.shape[0]
    n_chunks = t // CHUNK
    per_seq = n_chunks // batch
    chunk = lambda b, i: b * per_seq + i
    prev = lambda b, i: b * per_seq + jnp.maximum(i - 1, 0)

    def cur(width):
        return pl.BlockSpec((CHUNK, width), lambda b, i: (chunk(b, i), 0))

    def tail(piece, width, col):
        return pl.BlockSpec((None, MAX_DIL, None, piece, width),
                            lambda b, i: (prev(b, i), 0, BLOCK // piece - 1, 0, col))

    def tail_view(x, piece):
        return x.reshape(n_chunks, MAX_DIL, BLOCK // piece, piece, x.shape[-1])

    piece0 = BLOCK // MAX_DIL
    piece1 = BLOCK // (MAX_DIL // DIL_PAIRS[1][1])
    prev2 = pl.BlockSpec((None, CHUNK, A_OUT), lambda b, i: (prev(b, i), 0, 1))
    view2 = lambda x: x.reshape(n_chunks, CHUNK, A12_WIDTH)
    return pl.pallas_call(
        _dilated_kernel,
        out_shape=jax.ShapeDtypeStruct((t, A_OUT), BF16),
        grid=(batch, per_seq),
        in_specs=[cur(A_OUT), cur(A_OUT), cur(A_OUT), tail(piece0, A_OUT, 0), tail(piece0, A_OUT, 0),
                  cur(A12_WIDTH), cur(A12_WIDTH), cur(A12_WIDTH),
                  tail(piece1, A_OUT, 0), tail(piece1, A_OUT, 0), prev2, prev2],
        out_specs=cur(A_OUT),
        scratch_shapes=[pltpu.VMEM((2, CHUNK, A_OUT), F32), pltpu.VMEM((2, CHUNK, A_OUT), F32)],
        compiler_params=_params(2),
        name="dilated",
    )(q0, k0, v0, tail_view(k0, piece0), tail_view(v0, piece0),
      q12, k12, v12, tail_view(k12, piece1), tail_view(v12, piece1), view2(k12), view2(v12))


def _merge_kernel(h_ref, u_ref, oa_ref, ob_ref, om_ref, unperm_ref,
                  wg_ref, bg_ref, woa_ref, wob_ref, wom_ref, wout_ref, gpost_ref, out_ref):
    for tile in range(STEP_TILES):
        rows = slice(tile * ROW_TILE, (tile + 1) * ROW_TILE)
        u = u_ref[rows, :]

        def gate(idx):
            cols = slice(idx * D_MODEL, (idx + 1) * D_MODEL)
            z = jnp.dot(u, wg_ref[:, cols], preferred_element_type=F32) + bg_ref[:, cols]
            return 1.0 / (1.0 + jnp.exp(-z))

        oa_res = jnp.concatenate([oa_ref[c, tile] for c in range(MAX_DIL)], axis=0)
        o_a = jnp.dot(unperm_ref[...], oa_res, preferred_element_type=F32).astype(BF16)

        merged = gate(0) * jnp.dot(o_a, woa_ref[...], preferred_element_type=F32)
        merged = merged + gate(1) * jnp.dot(ob_ref[rows, :], wob_ref[...], preferred_element_type=F32)
        merged = merged + gate(2) * jnp.dot(om_ref[rows, :], wom_ref[...], preferred_element_type=F32)
        mixed = jnp.dot(merged.astype(BF16), wout_ref[...], preferred_element_type=F32)
        out_ref[rows, :] = h_ref[rows, :] + _rms(mixed, gpost_ref[...])


def _merge(h, u, oa, ob, om, unperm, wg, bg, woa, wob, wom, wout, gpost):
    t = h.shape[0]

    def row(width):
        return pl.BlockSpec((STEP_ROWS, width), lambda i: (i, 0))

    in_specs = ([row(D_MODEL), row(D_MODEL), _res_spec(A_OUT), row(B_WIDTH), row(M_WIDTH),
                 _const_spec((ROW_TILE, ROW_TILE)),
                 _const_spec((D_MODEL, 3 * D_MODEL)), _const_spec((1, 3 * D_MODEL)),
                 _const_spec((A_OUT, D_MODEL)), _const_spec((B_WIDTH, D_MODEL)),
                 _const_spec((M_WIDTH, D_MODEL)), _const_spec((D_MODEL, D_MODEL)),
                 _const_spec((1, D_MODEL))])
    oa = oa.reshape(t // CHUNK, MAX_DIL, TILES_PER_CHUNK, TILE_RUN, A_OUT)
    return pl.pallas_call(
        _merge_kernel,
        out_shape=jax.ShapeDtypeStruct((t, D_MODEL), F32),
        grid=(t // STEP_ROWS,),
        in_specs=in_specs,
        out_specs=row(D_MODEL),
        compiler_params=_params(),
        name="merge",
    )(h, u, oa, ob, om, unperm, wg, bg, woa, wob, wom, wout, gpost)


def _rope_consts(seq):
    half = HEAD_DIM // 2
    inv = ROPE_THETA ** (-np.arange(half, dtype=np.float64) / half)
    inv = np.concatenate([inv, inv])[None, :]
    base = np.arange(0, seq, STEP_ROWS, dtype=np.float64)[:, None] * inv
    local = np.arange(STEP_ROWS, dtype=np.float64)[:, None] * inv
    local_res = (local.reshape(STEP_TILES, TILE_RUN, MAX_DIL, HEAD_DIM).transpose(0, 2, 1, 3)
                 .reshape(STEP_ROWS, HEAD_DIM))
    tables = (np.cos(base), np.sin(base), np.cos(local), np.sin(local), np.cos(local_res), np.sin(local_res))
    return tuple(jnp.asarray(tb, dtype=F32) for tb in tables)


def _residue_perm():
    out_row = np.arange(ROW_TILE)
    src = (out_row % TILE_RUN) * MAX_DIL + out_row // TILE_RUN
    return np.equal(src[:, None], np.arange(ROW_TILE)[None, :]).astype(np.float32)


def kernel(x, mem, ffn1_norm_pre, ffn1_w_in, ffn1_w_out, ffn1_norm_post, mix_norm_pre, w_in, sinks, mem_norm, w_mem_kv, w_gate, b_gate, w_o_a, w_o_b, w_o_m, w_out, mix_norm_post, ffn2_norm_pre, ffn2_w_in, ffn2_w_out, ffn2_norm_post):
    b, s, _ = x.shape
    depth = ffn1_w_in.shape[0]
    t = b * s
    rope_consts = _rope_consts(s)
    perm = jnp.asarray(_residue_perm(), dtype=BF16)
    unperm = jnp.asarray(_residue_perm().T, dtype=BF16)
    h = x.reshape(t, D_MODEL)
    for l in range(depth):
        bf = lambda w: w[l].astype(BF16)
        mk, mv = _mem_kv(mem, mem_norm[l][None], bf(w_mem_kv))
        h1, u = _ffn(h, ffn1_norm_pre[l][None], bf(ffn1_w_in), bf(ffn1_w_out),
                     ffn1_norm_post[l][None], mix_norm_pre[l][None])
        q0, k0, v0, q12, k12, v12, bq, bk, bv, om = _proj(u, bf(w_in), perm, rope_consts, mk, mv, s)

        flat = lambda a: a.reshape(t, a.shape[-1])
        oa = _dilated_attention(flat(q0), flat(k0), flat(v0), flat(q12), flat(k12), flat(v12), b)
        seq3 = lambda a: a.reshape(b, s, a.shape[-1])
        ob = _band_attention(seq3(bq), seq3(bk), seq3(bv), B_WINDOW - 1, sinks[l]).reshape(t, B_WIDTH)

        h2 = _merge(h1, u, oa, ob, om, unperm, bf(w_gate), b_gate[l][None], bf(w_o_a), bf(w_o_b),
                    bf(w_o_m), bf(w_out), mix_norm_post[l][None])
        (h,) = _ffn(h2, ffn2_norm_pre[l][None], bf(ffn2_w_in), bf(ffn2_w_out), ffn2_norm_post[l][None])
    return h.reshape(b, s, D_MODEL)
```

```python
import functools

import jax
import jax.numpy as jnp
import numpy as np
from jax import lax
from jax.experimental import pallas as pl
from jax.experimental.pallas import tpu as pltpu

D_MODEL = 1024
HEAD_DIM = 128
DIL_PAIRS = ((128, 1), (512, 4), (2048, 16))
A_HEADS_PER_GROUP = 2
N_GROUPS = len(DIL_PAIRS)
A_HEADS = A_HEADS_PER_GROUP * N_GROUPS
B_Q_HEADS = 4
B_KV_HEADS = 2
B_WINDOW = 128
M_HEADS = 4
D_FF = 2816
ROPE_THETA = 10000.0
BLOCK = 128
EPS = 1e-6
NEG_INF = -1e30

A_WIDTH = A_HEADS * HEAD_DIM
A_OUT = A_HEADS_PER_GROUP * HEAD_DIM
B_WIDTH = B_Q_HEADS * HEAD_DIM
B_KV_WIDTH = B_KV_HEADS * HEAD_DIM
M_WIDTH = M_HEADS * HEAD_DIM
D_IN = 3 * A_WIDTH + B_WIDTH + 2 * B_KV_WIDTH + M_WIDTH
OFF_AQ = 0
OFF_AK = A_WIDTH
OFF_AV = 2 * A_WIDTH
OFF_BQ = 3 * A_WIDTH
OFF_BK = OFF_BQ + B_WIDTH
OFF_BV = OFF_BK + B_KV_WIDTH
OFF_MQ = OFF_BV + B_KV_WIDTH
QK_SCALE = HEAD_DIM ** -0.5

MXU_N = 256
ROW_TILE = 512
STEP_TILES = 2
STEP_ROWS = STEP_TILES * ROW_TILE
FFN_TILE = 256
FFN_TILES = STEP_ROWS // FFN_TILE
ATT_ROWS = 1024
VMEM_LIMIT = 56 * 1024 * 1024

MAX_DIL = max(d for _, d in DIL_PAIRS)
CHUNK = MAX_DIL * BLOCK
TILES_PER_CHUNK = CHUNK // ROW_TILE
STEPS_PER_CHUNK = TILES_PER_CHUNK // STEP_TILES
TILE_RUN = ROW_TILE // MAX_DIL
A12_WIDTH = 2 * A_OUT

F32 = jnp.float32
BF16 = jnp.bfloat16
NT_DIMS = (((1,), (1,)), ((), ()))


def _rms(x, g):
    return x * lax.rsqrt(jnp.mean(x * x, axis=-1, keepdims=True) + EPS) * g


def _const_spec(shape):
    nd = len(shape)
    return pl.BlockSpec(shape, lambda *_: (0,) * nd, pipeline_mode=pl.Buffered(1))


def _params(n_axes=1):
    return pltpu.CompilerParams(
        dimension_semantics=("arbitrary",) * n_axes, vmem_limit_bytes=VMEM_LIMIT)


def _mem_kv_kernel(mem_ref, g_ref, w_ref, mk_ref, mv_ref):
    mn = _rms(mem_ref[0], g_ref[...]).astype(BF16)
    kv = jnp.dot(mn, w_ref[...], preferred_element_type=F32)
    mk_ref[0] = kv[:, :M_WIDTH].astype(BF16)
    mv_ref[0] = kv[:, M_WIDTH:].astype(BF16)


def _mem_kv(mem, g, w):
    b, n, _ = mem.shape
    out = jax.ShapeDtypeStruct((b, n, M_WIDTH), BF16)
    return pl.pallas_call(
        _mem_kv_kernel,
        out_shape=(out, out),
        grid=(b,),
        in_specs=[pl.BlockSpec((1, n, D_MODEL), lambda i: (i, 0, 0)),
                  _const_spec((1, D_MODEL)),
                  _const_spec((D_MODEL, 2 * M_WIDTH))],
        out_specs=(pl.BlockSpec((1, n, M_WIDTH), lambda i: (i, 0, 0)),
                   pl.BlockSpec((1, n, M_WIDTH), lambda i: (i, 0, 0))),
        compiler_params=_params(),
        name="mem_kv",
    )(mem, g, w)


def _ffn_kernel(x_ref, gpre_ref, win_ref, wout_ref, gpost_ref, *rest, emit_u):
    if emit_u:
        gnext_ref, h_ref, u_ref, xn_ref, act_ref, f_ref = rest
    else:
        h_ref, xn_ref, act_ref, f_ref = rest
    rows = lambda tile: slice(tile * FFN_TILE, (tile + 1) * FFN_TILE)

    def prologue(tile):
        xn_ref[tile] = _rms(x_ref[rows(tile), :], gpre_ref[...]).astype(BF16)

    def up_chunk(tile, c):
        lo = c * MXU_N
        xn = xn_ref[tile]
        gate = jnp.dot(xn, win_ref[:, lo:lo + MXU_N], preferred_element_type=F32)
        up = jnp.dot(xn, win_ref[:, D_FF + lo:D_FF + lo + MXU_N], preferred_element_type=F32)
        silu = gate * (1.0 / (1.0 + jnp.exp(-gate)))
        act_ref[tile, :, lo:lo + MXU_N] = (silu * up).astype(BF16)

    def epilogue(tile):
        h = x_ref[rows(tile), :] + 0.5 * _rms(f_ref[tile], gpost_ref[...])
        h_ref[rows(tile), :] = h
        if emit_u:
            u_ref[rows(tile), :] = _rms(h, gnext_ref[...]).astype(BF16)

    n_chunks = D_FF // MXU_N
    prologue(0)
    for c in range(n_chunks):
        up_chunk(0, c)
    for tile in range(FFN_TILES):
        if tile + 1 < FFN_TILES:
            prologue(tile + 1)
            up_chunk(tile + 1, 0)
        f_ref[tile] = jnp.dot(act_ref[tile], wout_ref[...], preferred_element_type=F32)
        epilogue(tile)
        if tile + 1 < FFN_TILES:
            for c in range(1, n_chunks):
                up_chunk(tile + 1, c)


def _ffn(x, gpre, w_in, w_out, gpost, gnext=None):
    t = x.shape[0]
    emit_u = gnext is not None
    row = pl.BlockSpec((STEP_ROWS, D_MODEL), lambda i: (i, 0))
    gain = _const_spec((1, D_MODEL))
    in_specs = [row, gain, _const_spec((D_MODEL, 2 * D_FF)), _const_spec((D_FF, D_MODEL)), gain]
    args = [x, gpre, w_in, w_out, gpost]
    out_shape = [jax.ShapeDtypeStruct((t, D_MODEL), F32)]
    out_specs = [row]
    if emit_u:
        in_specs.append(gain)
        args.append(gnext)
        out_shape.append(jax.ShapeDtypeStruct((t, D_MODEL), BF16))
        out_specs.append(row)
    return pl.pallas_call(
        functools.partial(_ffn_kernel, emit_u=emit_u),
        out_shape=tuple(out_shape),
        grid=(t // STEP_ROWS,),
        in_specs=in_specs,
        out_specs=tuple(out_specs),
        scratch_shapes=[pltpu.VMEM((FFN_TILES, FFN_TILE, D_MODEL), BF16),
                        pltpu.VMEM((FFN_TILES, FFN_TILE, D_FF), BF16),
                        pltpu.VMEM((FFN_TILES, FFN_TILE, D_MODEL), F32)],
        compiler_params=_params(),
        name="ffn_u" if emit_u else "ffn",
    )(*args)


def _proj_kernel(u_ref, w_ref, perm_ref, cb_ref, sb_ref, cl_ref, sl_ref, clr_ref, slr_ref, mk_ref, mv_ref,
                 q0_ref, k0_ref, v0_ref, q12_ref, k12_ref, v12_ref, bq_ref, bk_ref, bv_ref, om_ref,
                 *, steps_per_seq):
    step = pl.program_id(0) % steps_per_seq
    cb, sb = cb_ref[pl.ds(step, 1), :], sb_ref[pl.ds(step, 1), :]
    lane = lax.broadcasted_iota(jnp.int32, (1, HEAD_DIM), 1)
    sign = jnp.where(lane < HEAD_DIM // 2, -1.0, 1.0)

    def rope_tables(cl, sl, scale):
        cos = (cb * scale) * cl - (sb * scale) * sl
        sin = (sb * (sign * scale)) * cl + (cb * (sign * scale)) * sl
        return cos, sin

    for tile in range(STEP_TILES):
        rows = slice(tile * ROW_TILE, (tile + 1) * ROW_TILE)
        cl, sl, clr, slr = cl_ref[rows, :], sl_ref[rows, :], clr_ref[rows, :], slr_ref[rows, :]
        _proj_tile(tile, u_ref, w_ref, perm_ref,
                   rope_tables(cl, sl, QK_SCALE) + rope_tables(cl, sl, 1.0),
                   rope_tables(clr, slr, QK_SCALE) + rope_tables(clr, slr, 1.0), mk_ref, mv_ref,
                   q0_ref, k0_ref, v0_ref, q12_ref, k12_ref, v12_ref, bq_ref, bk_ref, bv_ref, om_ref)


def _proj_tile(tile, u_ref, w_ref, perm_ref, tables, tables_res, mk_ref, mv_ref,
               q0_ref, k0_ref, v0_ref, q12_ref, k12_ref, v12_ref, bq_ref, bk_ref, bv_ref, om_ref):
    rows = slice(tile * ROW_TILE, (tile + 1) * ROW_TILE)
    u = u_ref[rows, :]
    u_res = jnp.dot(perm_ref[...], u, preferred_element_type=F32).astype(BF16)

    def proj(lhs, lo):
        return jnp.dot(lhs, w_ref[:, lo:lo + MXU_N], preferred_element_type=F32)

    def rope(y, cos, sin):
        out = []
        for hh in range(MXU_N // HEAD_DIM):
            yh = y[:, hh * HEAD_DIM:(hh + 1) * HEAD_DIM]
            out.append(yh * cos + pltpu.roll(yh, HEAD_DIM // 2, 1) * sin)
        return jnp.concatenate(out, axis=1)

    def put_res(dst, col, y):
        for c in range(MAX_DIL):
            dst[c, tile, :, col:col + MXU_N] = y[c * TILE_RUN:(c + 1) * TILE_RUN].astype(dst.dtype)

    cqp, sqp, ckp, skp = tables_res
    for g in range(N_GROUPS):
        dst_q, dst_k, dst_v = (q0_ref, k0_ref, v0_ref) if g == 0 else (q12_ref, k12_ref, v12_ref)
        col = 0 if g == 0 else (g - 1) * A_OUT
        put_res(dst_q, col, rope(proj(u_res, OFF_AQ + g * A_OUT), cqp, sqp))
        put_res(dst_k, col, rope(proj(u_res, OFF_AK + g * A_OUT), ckp, skp))
        put_res(dst_v, col, proj(u_res, OFF_AV + g * A_OUT))

    cq, sq, ck, sk = tables
    for c in range(B_WIDTH // MXU_N):
        bq_ref[rows, c * MXU_N:(c + 1) * MXU_N] = rope(proj(u, OFF_BQ + c * MXU_N), cq, sq).astype(BF16)
    bk_ref[rows, :] = rope(proj(u, OFF_BK), ck, sk).astype(BF16)
    bv_ref[rows, :] = proj(u, OFF_BV).astype(BF16)

    for c in range(M_WIDTH // MXU_N):
        y = proj(u, OFF_MQ + c * MXU_N) * QK_SCALE
        for hh in range(MXU_N // HEAD_DIM):
            h = c * (MXU_N // HEAD_DIM) + hh
            cols = slice(h * HEAD_DIM, (h + 1) * HEAD_DIM)
            q = y[:, hh * HEAD_DIM:(hh + 1) * HEAD_DIM].astype(BF16)
            s = lax.dot_general(q, mk_ref[0, :, cols], NT_DIMS, preferred_element_type=F32)
            m = jnp.max(s, axis=-1, keepdims=True)
            p = jnp.exp(s - m)
            den = jnp.sum(p, axis=-1, keepdims=True)
            o = jnp.dot(p.astype(BF16), mv_ref[0, :, cols], preferred_element_type=F32)
            om_ref[rows, cols] = (o / den).astype(BF16)


def _res_spec(width):
    return pl.BlockSpec((None, MAX_DIL, STEP_TILES, TILE_RUN, width),
                        lambda i: (i // STEPS_PER_CHUNK, 0, i % STEPS_PER_CHUNK, 0, 0))


def _proj(u, w_in, perm, rope_consts, mk, mv, seq):
    t = u.shape[0]
    n_mem = mk.shape[1]
    tiles_per_seq = seq // STEP_ROWS

    def row(width):
        return pl.BlockSpec((STEP_ROWS, width), lambda i: (i, 0))

    def res_shape(width, dtype):
        return jax.ShapeDtypeStruct((t // CHUNK, MAX_DIL, TILES_PER_CHUNK, TILE_RUN, width), dtype)

    base_table = _const_spec((tiles_per_seq, HEAD_DIM))
    local_table = _const_spec((STEP_ROWS, HEAD_DIM))
    memb =pl.BlockSpec((1, n_mem, M_WIDTH), lambda i: (i // tiles_per_seq, 0, 0))
    out_shape = ([res_shape(A_OUT, F32)] * 3 + [res_shape(A12_WIDTH, BF16)] * 3
                 + [jax.ShapeDtypeStruct((t, w), BF16) for w in (B_WIDTH, B_KV_WIDTH, B_KV_WIDTH, M_WIDTH)])
    out_specs = ([_res_spec(A_OUT)] * 3 + [_res_spec(A12_WIDTH)] * 3
                 + [row(w) for w in (B_WIDTH, B_KV_WIDTH, B_KV_WIDTH, M_WIDTH)])
    return pl.pallas_call(
        functools.partial(_proj_kernel, steps_per_seq=tiles_per_seq),
        out_shape=tuple(out_shape),
        grid=(t // STEP_ROWS,),
        in_specs=[row(D_MODEL), _const_spec((D_MODEL, D_IN)), _const_spec((ROW_TILE, ROW_TILE))]
        + [base_table] * 2 + [local_table] * 4 + [memb, memb],
        out_specs=tuple(out_specs),
        compiler_params=_params(),
        name="proj",
    )(u, w_in, perm, *rope_consts, mk, mv)


def _band_kernel(sink_ref, q_ref, k_ref, v_ref, kp_ref, vp_ref, o_ref, *, hq, hkv, max_dist, n_blk):
    grp = hq // hkv
    first_tile = pl.program_id(1) == 0

    row = lax.broadcasted_iota(jnp.int32, (BLOCK, 2 * BLOCK), 0)
    col = lax.broadcasted_iota(jnp.int32, (BLOCK, 2 * BLOCK), 1)
    dist = row + BLOCK - col
    band = (dist >= 0) & (dist <= max_dist)
    band_first = band & ((col >= BLOCK) | jnp.logical_not(first_tile))

    for j in range(n_blk):
        rows = slice(j * BLOCK, (j + 1) * BLOCK)
        mask = band_first if j == 0 else band
        for hk in range(hkv):
            kcols = slice(hk * HEAD_DIM, (hk + 1) * HEAD_DIM)
            if j == 0:
                kk = jnp.concatenate([kp_ref[0, :, kcols], k_ref[0, rows, kcols]], axis=0)
                vv = jnp.concatenate([vp_ref[0, :, kcols], v_ref[0, rows, kcols]], axis=0)
            else:
                both = slice((j - 1) * BLOCK, (j + 1) * BLOCK)
                kk = k_ref[0, both, kcols]
                vv = v_ref[0, both, kcols]
            for g in range(grp):
                h = hk * grp + g
                qcols = slice(h * HEAD_DIM, (h + 1) * HEAD_DIM)
                s = lax.dot_general(q_ref[0, rows, qcols], kk, NT_DIMS, preferred_element_type=F32)
                s = jnp.where(mask, s, NEG_INF)
                sk = sink_ref[h]
                m = jnp.maximum(jnp.max(s, axis=-1, keepdims=True), sk)
                p = jnp.exp(s - m)
                tot = jnp.sum(p, axis=-1, keepdims=True) + jnp.exp(sk - m)
                o = jnp.dot(p.astype(BF16), vv, preferred_element_type=F32)
                o_ref[0, rows, qcols] = (o / tot).astype(BF16)


def _band_attention(q, k, v, max_dist, sink):
    n, length, qw = q.shape
    kw = k.shape[2]
    rows = min(ATT_ROWS, length)
    n_blk = rows // BLOCK
    cur = lambda w: pl.BlockSpec((1, rows, w), lambda b, i: (b, i, 0))
    prev = pl.BlockSpec((1, BLOCK, kw), lambda b, i: (b, jnp.maximum(i * n_blk - 1, 0), 0))
    return pl.pallas_call(
        functools.partial(_band_kernel, hq=qw // HEAD_DIM, hkv=kw // HEAD_DIM, max_dist=max_dist,
                          n_blk=n_blk),
        out_shape=jax.ShapeDtypeStruct((n, length, qw), BF16),
        grid=(n, length // rows),
        in_specs=[pl.BlockSpec(memory_space=pltpu.SMEM), cur(qw), cur(kw), cur(kw), prev, prev],
        out_specs=cur(qw),
        compiler_params=_params(2),
        name="band_sink",
    )(sink, q, k, v, k, v)


def _dilated_kernel(q0_ref, k0_ref, v0_ref, k0p_ref, v0p_ref,
                    q12_ref, k12_ref, v12_ref, k1p_ref, v1p_ref, k2p_ref, v2p_ref,
                    o_ref, og_ref, lg_ref):
    first_chunk = pl.program_id(1) == 0
    row = lax.broadcasted_iota(jnp.int32, (BLOCK, 2 * BLOCK), 0)
    col = lax.broadcasted_iota(jnp.int32, (BLOCK, 2 * BLOCK), 1)
    own = col >= BLOCK
    kcol = jnp.where(own, col - BLOCK, col)

    def masks(n_pieces):
        piece = BLOCK // n_pieces
        shift = piece.bit_length() - 1
        pos = lambda r: (r & (piece - 1)) * n_pieces + (r >> shift)
        dist = pos(row) - pos(kcol) + jnp.where(own, 0, BLOCK)
        band = (dist >= 0) & (dist <= BLOCK)
        return band, band & (own | jnp.logical_not(first_chunk))

    def attend(q, kk, vv, mask):
        s = lax.dot_general(q, kk, NT_DIMS, preferred_element_type=F32)
        s = jnp.where(mask, s, NEG_INF)
        m = jnp.max(s, axis=-1, keepdims=True)
        p = jnp.exp(s - m)
        den = jnp.sum(p, axis=-1, keepdims=True)
        o = jnp.dot(p.astype(BF16), vv, preferred_element_type=F32) / den
        return o, jnp.broadcast_to(m + jnp.log(den), (BLOCK, HEAD_DIM))

    def gather(ref, starts, piece, cols):
        return jnp.concatenate([ref[s:s + piece, cols] for s in starts], axis=0).astype(BF16)

    for g, n_pieces in ((0, MAX_DIL), (1, MAX_DIL // DIL_PAIRS[1][1])):
        piece = BLOCK // n_pieces
        band, band_first = masks(n_pieces)
        q_ref, k_ref, v_ref, kp_ref, vp_ref = ((q0_ref, k0_ref, v0_ref, k0p_ref, v0p_ref) if g == 0 else
                                               (q12_ref, k12_ref, v12_ref, k1p_ref, v1p_ref))
        n_res = MAX_DIL // n_pieces
        for res in range(n_res):
            for h in range(A_HEADS_PER_GROUP):
                cols = slice(h * HEAD_DIM, (h + 1) * HEAD_DIM)
                bases = [(res + n_res * p) * BLOCK for p in range(n_pieces)]
                k_prev = jnp.concatenate([kp_ref[res + n_res * p, :, cols] for p in range(n_pieces)],
                                         axis=0).astype(BF16)
                v_prev = jnp.concatenate([vp_ref[res + n_res * p, :, cols] for p in range(n_pieces)],
                                         axis=0).astype(BF16)
                for a in range(BLOCK // piece):
                    starts = [b0 + a * piece for b0 in bases]
                    k_own = gather(k_ref, starts, piece, cols)
                    v_own = gather(v_ref, starts, piece, cols)
                    o, lse = attend(gather(q_ref, starts, piece, cols),
                                    jnp.concatenate([k_prev, k_own], axis=0),
                                    jnp.concatenate([v_prev, v_own], axis=0),
                                    band_first if a == 0 else band)
                    for p, s0 in enumerate(starts):
                        og_ref[g, s0:s0 + piece, cols] = o[p * piece:(p + 1) * piece]
                        lg_ref[g, s0:s0 + piece, cols] = lse[p * piece:(p + 1) * piece]
                    k_prev, v_prev = k_own, v_own

    band, band_first = masks(1)
    del band
    for c in range(MAX_DIL):
        rows = slice(c * BLOCK, (c + 1) * BLOCK)
        for h in range(A_HEADS_PER_GROUP):
            cols = slice(h * HEAD_DIM, (h + 1) * HEAD_DIM)
            cols2 = slice(A_OUT + h * HEAD_DIM, A_OUT + (h + 1) * HEAD_DIM)
            o2, l2 = attend(q12_ref[rows, cols2],
                            jnp.concatenate([k2p_ref[rows, cols], k12_ref[rows, cols2]], axis=0),
                            jnp.concatenate([v2p_ref[rows, cols], v12_ref[rows, cols2]], axis=0),
                            band_first)
            l0, l1 = lg_ref[0, rows, cols], lg_ref[1, rows, cols]
            mx = jnp.maximum(jnp.maximum(l0, l1), l2)
            e0, e1, e2 = jnp.exp(l0 - mx), jnp.exp(l1 - mx), jnp.exp(l2 - mx)
            den = e0 + e1 + e2
            acc = (e0 / den) * og_ref[0, rows, cols] + (e1 / den) * og_ref[1, rows, cols] + (e2 / den) * o2
            o_ref[rows, cols] = acc.astype(BF16)


def _dilated_attention(q0, k0, v0, q12, k12, v12, batch):
    t = q0.shape[0]
    n_chunks = t // CHUNK
    per_seq = n_chunks // batch
    chunk = lambda b, i: b * per_seq + i
    prev = lambda b, i: b * per_seq + jnp.maximum(i - 1, 0)

    def cur(width):
        return pl.BlockSpec((CHUNK, width), lambda b, i: (chunk(b, i), 0))

    def tail(piece, width, col):
        return pl.BlockSpec((None, MAX_DIL, None, piece, width),
                            lambda b, i: (prev(b, i), 0, BLOCK // piece - 1, 0, col))

    def tail_view(x, piece):
        return x.reshape(n_chunks, MAX_DIL, BLOCK // piece, piece, x.shape[-1])

    piece0 = BLOCK // MAX_DIL
    piece1 = BLOCK // (MAX_DIL // DIL_PAIRS[1][1])
    prev2 = pl.BlockSpec((None, CHUNK, A_OUT), lambda b, i: (prev(b, i), 0, 1))
    view2 = lambda x: x.reshape(n_chunks, CHUNK, A12_WIDTH)
    return pl.pallas_call(
        _dilated_kernel,
        out_shape=jax.ShapeDtypeStruct((t, A_OUT), BF16),
        grid=(batch, per_seq),
        in_specs=[cur(A_OUT), cur(A_OUT), cur(A_OUT), tail(piece0, A_OUT, 0), tail(piece0, A_OUT, 0),
                  cur(A12_WIDTH), cur(A12_WIDTH), cur(A12_WIDTH),
                  tail(piece1, A_OUT, 0), tail(piece1, A_OUT, 0), prev2, prev2],
        out_specs=cur(A_OUT),
        scratch_shapes=[pltpu.VMEM((2, CHUNK, A_OUT), F32), pltpu.VMEM((2, CHUNK, A_OUT), F32)],
        compiler_params=_params(2),
        name="dilated",
    )(q0, k0, v0, tail_view(k0, piece0), tail_view(v0, piece0),
      q12, k12, v12, tail_view(k12, piece1), tail_view(v12, piece1), view2(k12), view2(v12))


def _merge_kernel(h_ref, u_ref, oa_ref, ob_ref, om_ref, unperm_ref,
                  wg_ref, bg_ref, woa_ref, wob_ref, wom_ref, wout_ref, gpost_ref, out_ref):
    for tile in range(STEP_TILES):
        rows = slice(tile * ROW_TILE, (tile + 1) * ROW_TILE)
        u = u_ref[rows, :]

        def gate(idx):
            cols = slice(idx * D_MODEL, (idx + 1) * D_MODEL)
            z = jnp.dot(u, wg_ref[:, cols], preferred_element_type=F32) + bg_ref[:, cols]
            return 1.0 / (1.0 + jnp.exp(-z))

        oa_res = jnp.concatenate([oa_ref[c, tile] for c in range(MAX_DIL)], axis=0)
        o_a = jnp.dot(unperm_ref[...], oa_res, preferred_element_type=F32).astype(BF16)

        merged = gate(0) * jnp.dot(o_a, woa_ref[...], preferred_element_type=F32)
        merged = merged + gate(1) * jnp.dot(ob_ref[rows, :], wob_ref[...], preferred_element_type=F32)
        merged = merged + gate(2) * jnp.dot(om_ref[rows, :], wom_ref[...], preferred_element_type=F32)
        mixed = jnp.dot(merged.astype(BF16), wout_ref[...], preferred_element_type=F32)
        out_ref[rows, :] = h_ref[rows, :] + _rms(mixed, gpost_ref[...])


def _merge(h, u, oa, ob, om, unperm, wg, bg, woa, wob, wom, wout, gpost):
    t = h.shape[0]

    def row(width):
        return pl.BlockSpec((STEP_ROWS, width), lambda i: (i, 0))

    in_specs = ([row(D_MODEL), row(D_MODEL), _res_spec(A_OUT), row(B_WIDTH), row(M_WIDTH),
                 _const_spec((ROW_TILE, ROW_TILE)),
                 _const_spec((D_MODEL, 3 * D_MODEL)), _const_spec((1, 3 * D_MODEL)),
                 _const_spec((A_OUT, D_MODEL)), _const_spec((B_WIDTH, D_MODEL)),
                 _const_spec((M_WIDTH, D_MODEL)), _const_spec((D_MODEL, D_MODEL)),
                 _const_spec((1, D_MODEL))])
    oa = oa.reshape(t // CHUNK, MAX_DIL, TILES_PER_CHUNK, TILE_RUN, A_OUT)
    return pl.pallas_call(
        _merge_kernel,
        out_shape=jax.ShapeDtypeStruct((t, D_MODEL), F32),
        grid=(t // STEP_ROWS,),
        in_specs=in_specs,
        out_specs=row(D_MODEL),
        compiler_params=_params(),
        name="merge",
    )(h, u, oa, ob, om, unperm, wg, bg, woa, wob, wom, wout, gpost)


def _rope_consts(seq):
    half = HEAD_DIM // 2
    inv = ROPE_THETA ** (-np.arange(half, dtype=np.float64) / half)
    inv = np.concatenate([inv, inv])[None, :]
    base = np.arange(0, seq, STEP_ROWS, dtype=np.float64)[:, None] * inv
    local = np.arange(STEP_ROWS, dtype=np.float64)[:, None] * inv
    local_res = (local.reshape(STEP_TILES, TILE_RUN, MAX_DIL, HEAD_DIM).transpose(0, 2, 1, 3)
                 .reshape(STEP_ROWS, HEAD_DIM))
    tables = (np.cos(base), np.sin(base), np.cos(local), np.sin(local), np.cos(local_res), np.sin(local_res))
    return tuple(jnp.asarray(tb, dtype=F32) for tb in tables)


def _residue_perm():
    out_row = np.arange(ROW_TILE)
    src = (out_row % TILE_RUN) * MAX_DIL + out_row // TILE_RUN
    return np.equal(src[:, None], np.arange(ROW_TILE)[None, :]).astype(np.float32)


def kernel(x, mem, ffn1_norm_pre, ffn1_w_in, ffn1_w_out, ffn1_norm_post, mix_norm_pre, w_in, sinks, mem_norm, w_mem_kv, w_gate, b_gate, w_o_a, w_o_b, w_o_m, w_out, mix_norm_post, ffn2_norm_pre, ffn2_w_in, ffn2_w_out, ffn2_norm_post):
    b, s, _ = x.shape
    depth = ffn1_w_in.shape[0]
    t = b * s
    rope_consts = _rope_consts(s)
    perm = jnp.asarray(_residue_perm(), dtype=BF16)
    unperm = jnp.asarray(_residue_perm().T, dtype=BF16)
    h = x.reshape(t, D_MODEL)
    for l in range(depth):
        bf = lambda w: w[l].astype(BF16)
        mk, mv = _mem_kv(mem, mem_norm[l][None], bf(w_mem_kv))
        h1, u = _ffn(h, ffn1_norm_pre[l][None], bf(ffn1_w_in), bf(ffn1_w_out),
                     ffn1_norm_post[l][None], mix_norm_pre[l][None])
        q0, k0, v0, q12, k12, v12, bq, bk, bv, om = _proj(u, bf(w_in), perm, rope_consts, mk, mv, s)

        flat = lambda a: a.reshape(t, a.shape[-1])
        oa = _dilated_attention(flat(q0), flat(k0), flat(v0), flat(q12), flat(k12), flat(v12), b)
        seq3 = lambda a: a.reshape(b, s, a.shape[-1])
        ob = _band_attention(seq3(bq), seq3(bk), seq3(bv), B_WINDOW - 1, sinks[l]).reshape(t, B_WIDTH)

        h2 = _merge(h1, u, oa, ob, om, unperm, bf(w_gate), b_gate[l][None], bf(w_o_a), bf(w_o_b),
                    bf(w_o_m), bf(w_out), mix_norm_post[l][None])
        (h,) = _ffn(h2, ffn2_norm_pre[l][None], bf(ffn2_w_in), bf(ffn2_w_out), ffn2_norm_post[l][None])
    return h.reshape(b, s, D_MODEL)
```

```python
import functools

import jax
import jax.numpy as jnp
import numpy as np
from jax import lax
from jax.experimental import pallas as pl
from jax.experimental.pallas import tpu as pltpu

D_MODEL = 1024
HEAD_DIM = 128
DIL_PAIRS = ((128, 1), (512, 4), (2048, 16))
A_HEADS_PER_GROUP = 2
N_GROUPS = len(DIL_PAIRS)
A_HEADS = A_HEADS_PER_GROUP * N_GROUPS
B_Q_HEADS = 4
B_KV_HEADS = 2
B_WINDOW = 128
M_HEADS = 4
D_FF = 2816
ROPE_THETA = 10000.0
BLOCK = 128
EPS = 1e-6
NEG_INF = -1e30

A_WIDTH = A_HEADS * HEAD_DIM
A_OUT = A_HEADS_PER_GROUP * HEAD_DIM
B_WIDTH = B_Q_HEADS * HEAD_DIM
B_KV_WIDTH = B_KV_HEADS * HEAD_DIM
M_WIDTH = M_HEADS * HEAD_DIM
D_IN = 3 * A_WIDTH + B_WIDTH + 2 * B_KV_WIDTH + M_WIDTH
OFF_AQ = 0
OFF_AK = A_WIDTH
OFF_AV = 2 * A_WIDTH
OFF_BQ = 3 * A_WIDTH
OFF_BK = OFF_BQ + B_WIDTH
OFF_BV = OFF_BK + B_KV_WIDTH
OFF_MQ = OFF_BV + B_KV_WIDTH
QK_SCALE = HEAD_DIM ** -0.5
LOG2E = 1.4426950408889634
Q_SCALE = QK_SCALE * LOG2E

MXU_N = 256
ROW_TILE = 512
STEP_TILES = 2
STEP_ROWS = STEP_TILES * ROW_TILE
FFN_TILE = 256
FFN_TILES = STEP_ROWS // FFN_TILE
ATT_ROWS = 1024
VMEM_LIMIT = 56 * 1024 * 1024

MAX_DIL = max(d for _, d in DIL_PAIRS)
CHUNK = MAX_DIL * BLOCK
TILES_PER_CHUNK = CHUNK // ROW_TILE
STEPS_PER_CHUNK = TILES_PER_CHUNK // STEP_TILES
TILE_RUN = ROW_TILE // MAX_DIL
A12_WIDTH = 2 * A_OUT

F32 = jnp.float32
BF16 = jnp.bfloat16
NT_DIMS = (((1,), (1,)), ((), ()))


def _rms(x, g):
    return x * lax.rsqrt(jnp.mean(x * x, axis=-1, keepdims=True) + EPS) * g


def _const_spec(shape):
    nd = len(shape)
    return pl.BlockSpec(shape, lambda *_: (0,) * nd, pipeline_mode=pl.Buffered(1))


def _params(n_axes=1):
    return pltpu.CompilerParams(
        dimension_semantics=("arbitrary",) * n_axes, vmem_limit_bytes=VMEM_LIMIT)


def _mem_kv_kernel(mem_ref, g_ref, w_ref, mk_ref, mv_ref):
    mn = _rms(mem_ref[0], g_ref[...]).astype(BF16)
    kv = jnp.dot(mn, w_ref[...], preferred_element_type=F32)
    mk_ref[0] = kv[:, :M_WIDTH].astype(BF16)
    mv_ref[0] = kv[:, M_WIDTH:].astype(BF16)


def _mem_kv(mem, g, w):
    b, n, _ = mem.shape
    out = jax.ShapeDtypeStruct((b, n, M_WIDTH), BF16)
    return pl.pallas_call(
        _mem_kv_kernel,
        out_shape=(out, out),
        grid=(b,),
        in_specs=[pl.BlockSpec((1, n, D_MODEL), lambda i: (i, 0, 0)),
                  _const_spec((1, D_MODEL)),
                  _const_spec((D_MODEL, 2 * M_WIDTH))],
        out_specs=(pl.BlockSpec((1, n, M_WIDTH), lambda i: (i, 0, 0)),
                   pl.BlockSpec((1, n, M_WIDTH), lambda i: (i, 0, 0))),
        compiler_params=_params(),
        name="mem_kv",
    )(mem, g, w)


def _ffn_kernel(x_ref, gpre_ref, win_ref, wout_ref, gpost_ref, *rest, emit_u):
    if emit_u:
        gnext_ref, h_ref, u_ref, xn_ref, act_ref, f_ref = rest
    else:
        h_ref, xn_ref, act_ref, f_ref = rest
    rows = lambda tile: slice(tile * FFN_TILE, (tile + 1) * FFN_TILE)

    def prologue(tile):
        xn_ref[tile] = _rms(x_ref[rows(tile), :], gpre_ref[...]).astype(BF16)

    def up_chunk(tile, c):
        lo = c * MXU_N
        xn = xn_ref[tile]
        gate = jnp.dot(xn, win_ref[:, lo:lo + MXU_N], preferred_element_type=F32)
        up = jnp.dot(xn, win_ref[:, D_FF + lo:D_FF + lo + MXU_N], preferred_element_type=F32)
        silu = gate * (1.0 / (1.0 + jnp.exp(-gate)))
        act_ref[tile, :, lo:lo + MXU_N] = (silu * up).astype(BF16)

    def epilogue(tile):
        h = x_ref[rows(tile), :] + 0.5 * _rms(f_ref[tile], gpost_ref[...])
        h_ref[rows(tile), :] = h
        if emit_u:
            u_ref[rows(tile), :] = _rms(h, gnext_ref[...]).astype(BF16)

    n_chunks = D_FF // MXU_N
    prologue(0)
    for c in range(n_chunks):
        up_chunk(0, c)
    for tile in range(FFN_TILES):
        if tile + 1 < FFN_TILES:
            prologue(tile + 1)
            up_chunk(tile + 1, 0)
        f_ref[tile] = jnp.dot(act_ref[tile], wout_ref[...], preferred_element_type=F32)
        epilogue(tile)
        if tile + 1 < FFN_TILES:
            for c in range(1, n_chunks):
                up_chunk(tile + 1, c)


def _ffn(x, gpre, w_in, w_out, gpost, gnext=None):
    t = x.shape[0]
    emit_u = gnext is not None
    row = pl.BlockSpec((STEP_ROWS, D_MODEL), lambda i: (i, 0))
    gain = _const_spec((1, D_MODEL))
    in_specs = [row, gain, _const_spec((D_MODEL, 2 * D_FF)), _const_spec((D_FF, D_MODEL)), gain]
    args = [x, gpre, w_in, w_out, gpost]
    out_shape = [jax.ShapeDtypeStruct((t, D_MODEL), F32)]
    out_specs = [row]
    if emit_u:
        in_specs.append(gain)
        args.append(gnext)
        out_shape.append(jax.ShapeDtypeStruct((t, D_MODEL), BF16))
        out_specs.append(row)
    return pl.pallas_call(
        functools.partial(_ffn_kernel, emit_u=emit_u),
        out_shape=tuple(out_shape),
        grid=(t // STEP_ROWS,),
        in_specs=in_specs,
        out_specs=tuple(out_specs),
        scratch_shapes=[pltpu.VMEM((FFN_TILES, FFN_TILE, D_MODEL), BF16),
                        pltpu.VMEM((FFN_TILES, FFN_TILE, D_FF), BF16),
                        pltpu.VMEM((FFN_TILES, FFN_TILE, D_MODEL), F32)],
        compiler_params=_params(),
        name="ffn_u" if emit_u else "ffn",
    )(*args)


def _proj_kernel(u_ref, w_ref, perm_ref, cb_ref, sb_ref, cl_ref, sl_ref, clr_ref, slr_ref, mk_ref, mv_ref,
                 q0_ref, k0_ref, v0_ref, q12_ref, k12_ref, v12_ref, bq_ref, bk_ref, bv_ref, om_ref,
                 *, steps_per_seq):
    step = pl.program_id(0) % steps_per_seq
    cb, sb = cb_ref[pl.ds(step, 1), :], sb_ref[pl.ds(step, 1), :]
    lane = lax.broadcasted_iota(jnp.int32, (1, HEAD_DIM), 1)
    sign = jnp.where(lane < HEAD_DIM // 2, -1.0, 1.0)

    def rope_tables(cl, sl, scale):
        cos = (cb * scale) * cl - (sb * scale) * sl
        sin = (sb * (sign * scale)) * cl + (cb * (sign * scale)) * sl
        return cos, sin

    for tile in range(STEP_TILES):
        rows = slice(tile * ROW_TILE, (tile + 1) * ROW_TILE)
        cl, sl, clr, slr = cl_ref[rows, :], sl_ref[rows, :], clr_ref[rows, :], slr_ref[rows, :]
        _proj_tile(tile, u_ref, w_ref, perm_ref,
                   rope_tables(cl, sl, Q_SCALE) + rope_tables(cl, sl, 1.0),
                   rope_tables(clr, slr, Q_SCALE) + rope_tables(clr, slr, 1.0), mk_ref, mv_ref,
                   q0_ref, k0_ref, v0_ref, q12_ref, k12_ref, v12_ref, bq_ref, bk_ref, bv_ref, om_ref)


def _proj_tile(tile, u_ref, w_ref, perm_ref, tables, tables_res, mk_ref, mv_ref,
               q0_ref, k0_ref, v0_ref, q12_ref, k12_ref, v12_ref, bq_ref, bk_ref, bv_ref, om_ref):
    rows = slice(tile * ROW_TILE, (tile + 1) * ROW_TILE)
    u = u_ref[rows, :]
    u_res = jnp.dot(perm_ref[...], u, preferred_element_type=F32).astype(BF16)

    def proj(lhs, lo):
        return jnp.dot(lhs, w_ref[:, lo:lo + MXU_N], preferred_element_type=F32)

    def rope(y, cos, sin):
        out = []
        for hh in range(MXU_N // HEAD_DIM):
            yh = y[:, hh * HEAD_DIM:(hh + 1) * HEAD_DIM]
            out.append(yh * cos + pltpu.roll(yh, HEAD_DIM // 2, 1) * sin)
        return jnp.concatenate(out, axis=1)

    def put_res(dst, col, y):
        for c in range(MAX_DIL):
            dst[c, tile, :, col:col + MXU_N] = y[c * TILE_RUN:(c + 1) * TILE_RUN].astype(dst.dtype)

    cqp, sqp, ckp, skp = tables_res
    for g in range(N_GROUPS):
        dst_q, dst_k, dst_v = (q0_ref, k0_ref, v0_ref) if g == 0 else (q12_ref, k12_ref, v12_ref)
        col = 0 if g == 0 else (g - 1) * A_OUT
        put_res(dst_q, col, rope(proj(u_res, OFF_AQ + g * A_OUT), cqp, sqp))
        put_res(dst_k, col, rope(proj(u_res, OFF_AK + g * A_OUT), ckp, skp))
        put_res(dst_v, col, proj(u_res, OFF_AV + g * A_OUT))

    cq, sq, ck, sk = tables
    for c in range(B_WIDTH // MXU_N):
        bq_ref[rows, c * MXU_N:(c + 1) * MXU_N] = rope(proj(u, OFF_BQ + c * MXU_N), cq, sq).astype(BF16)
    bk_ref[rows, :] = rope(proj(u, OFF_BK), ck, sk).astype(BF16)
    bv_ref[rows, :] = proj(u, OFF_BV).astype(BF16)

    for c in range(M_WIDTH // MXU_N):
        y = proj(u, OFF_MQ + c * MXU_N) * Q_SCALE
        for hh in range(MXU_N // HEAD_DIM):
            h = c * (MXU_N // HEAD_DIM) + hh
            cols = slice(h * HEAD_DIM, (h + 1) * HEAD_DIM)
            q = y[:, hh * HEAD_DIM:(hh + 1) * HEAD_DIM].astype(BF16)
            s = lax.dot_general(q, mk_ref[0, :, cols], NT_DIMS, preferred_element_type=F32)
            m = jnp.max(s, axis=-1, keepdims=True)
            p = jnp.exp2(s - m)
            den = jnp.sum(p, axis=-1, keepdims=True)
            o = jnp.dot(p.astype(BF16), mv_ref[0, :, cols], preferred_element_type=F32)
            om_ref[rows, cols] = (o / den).astype(BF16)


def _res_spec(width):
    return pl.BlockSpec((None, MAX_DIL, STEP_TILES, TILE_RUN, width),
                        lambda i: (i // STEPS_PER_CHUNK, 0, i % STEPS_PER_CHUNK, 0, 0))


def _proj(u, w_in, perm, rope_consts, mk, mv, seq):
    t = u.shape[0]
    n_mem = mk.shape[1]
    tiles_per_seq = seq // STEP_ROWS

    def row(width):
        return pl.BlockSpec((STEP_ROWS, width), lambda i: (i, 0))

    def res_shape(width, dtype):
        return jax.ShapeDtypeStruct((t // CHUNK, MAX_DIL, TILES_PER_CHUNK, TILE_RUN, width), dtype)

    base_table = _const_spec((tiles_per_seq, HEAD_DIM))
    local_table = _const_spec((STEP_ROWS, HEAD_DIM))
    memb =pl.BlockSpec((1, n_mem, M_WIDTH), lambda i: (i // tiles_per_seq, 0, 0))
    out_shape = ([res_shape(A_OUT, F32)] * 3 + [res_shape(A12_WIDTH, BF16)] * 3
                 + [jax.ShapeDtypeStruct((t, w), BF16) for w in (B_WIDTH, B_KV_WIDTH, B_KV_WIDTH, M_WIDTH)])
    out_specs = ([_res_spec(A_OUT)] * 3 + [_res_spec(A12_WIDTH)] * 3
                 + [row(w) for w in (B_WIDTH, B_KV_WIDTH, B_KV_WIDTH, M_WIDTH)])
    return pl.pallas_call(
        functools.partial(_proj_kernel, steps_per_seq=tiles_per_seq),
        out_shape=tuple(out_shape),
        grid=(t // STEP_ROWS,),
        in_specs=[row(D_MODEL), _const_spec((D_MODEL, D_IN)), _const_spec((ROW_TILE, ROW_TILE))]
        + [base_table] * 2 + [local_table] * 4 + [memb, memb],
        out_specs=tuple(out_specs),
        compiler_params=_params(),
        name="proj",
    )(u, w_in, perm, *rope_consts, mk, mv)


def _band_kernel(sink_ref, q_ref, k_ref, v_ref, kp_ref, vp_ref, o_ref, *, hq, hkv, max_dist, n_blk):
    grp = hq // hkv
    first_tile = pl.program_id(1) == 0

    row = lax.broadcasted_iota(jnp.int32, (BLOCK, 2 * BLOCK), 0)
    col = lax.broadcasted_iota(jnp.int32, (BLOCK, 2 * BLOCK), 1)
    dist = row + BLOCK - col
    band = (dist >= 0) & (dist <= max_dist)
    band_first = band & ((col >= BLOCK) | jnp.logical_not(first_tile))

    for j in range(n_blk):
        rows = slice(j * BLOCK, (j + 1) * BLOCK)
        mask = band_first if j == 0 else band
        for hk in range(hkv):
            kcols = slice(hk * HEAD_DIM, (hk + 1) * HEAD_DIM)
            if j == 0:
                kk = jnp.concatenate([kp_ref[0, :, kcols], k_ref[0, rows, kcols]], axis=0)
                vv = jnp.concatenate([vp_ref[0, :, kcols], v_ref[0, rows, kcols]], axis=0)
            else:
                both = slice((j - 1) * BLOCK, (j + 1) * BLOCK)
                kk = k_ref[0, both, kcols]
                vv = v_ref[0, both, kcols]
            for g in range(grp):
                h = hk * grp + g
                qcols = slice(h * HEAD_DIM, (h + 1) * HEAD_DIM)
                s = lax.dot_general(q_ref[0, rows, qcols], kk, NT_DIMS, preferred_element_type=F32)
                s = jnp.where(mask, s, NEG_INF)
                sk = sink_ref[h] * LOG2E
                m = jnp.maximum(jnp.max(s, axis=-1, keepdims=True), sk)
                p = jnp.exp2(s - m)
                tot = jnp.sum(p, axis=-1, keepdims=True) + jnp.exp2(sk - m)
                o = jnp.dot(p.astype(BF16), vv, preferred_element_type=F32)
                o_ref[0, rows, qcols] = (o / tot).astype(BF16)


def _band_attention(q, k, v, max_dist, sink):
    n, length, qw = q.shape
    kw = k.shape[2]
    rows = min(ATT_ROWS, length)
    n_blk = rows // BLOCK
    cur = lambda w: pl.BlockSpec((1, rows, w), lambda b, i: (b, i, 0))
    prev = pl.BlockSpec((1, BLOCK, kw), lambda b, i: (b, jnp.maximum(i * n_blk - 1, 0), 0))
    return pl.pallas_call(
        functools.partial(_band_kernel, hq=qw // HEAD_DIM, hkv=kw // HEAD_DIM, max_dist=max_dist,
                          n_blk=n_blk),
        out_shape=jax.ShapeDtypeStruct((n, length, qw), BF16),
        grid=(n, length // rows),
        in_specs=[pl.BlockSpec(memory_space=pltpu.SMEM), cur(qw), cur(kw), cur(kw), prev, prev],
        out_specs=cur(qw),
        compiler_params=_params(2),
        name="band_sink",
    )(sink, q, k, v, k, v)


def _dilated_kernel(q0_ref, k0_ref, v0_ref, k0p_ref, v0p_ref,
                    q12_ref, k12_ref, v12_ref, k1p_ref, v1p_ref, k2p_ref, v2p_ref,
                    o_ref, og_ref, lg_ref):
    first_chunk = pl.program_id(1) == 0
    row = lax.broadcasted_iota(jnp.int32, (BLOCK, 2 * BLOCK), 0)
    col = lax.broadcasted_iota(jnp.int32, (BLOCK, 2 * BLOCK), 1)
    own = col >= BLOCK
    kcol = jnp.where(own, col - BLOCK, col)

    def masks(n_pieces):
        piece = BLOCK // n_pieces
        shift = piece.bit_length() - 1
        pos = lambda r: (r & (piece - 1)) * n_pieces + (r >> shift)
        dist = pos(row) - pos(kcol) + jnp.where(own, 0, BLOCK)
        band = (dist >= 0) & (dist <= BLOCK)
        return band, band & (own | jnp.logical_not(first_chunk))

    def attend(q, kk, vv, mask):
        s = lax.dot_general(q, kk, NT_DIMS, preferred_element_type=F32)
        s = jnp.where(mask, s, NEG_INF)
        m = jnp.max(s, axis=-1, keepdims=True)
        p = jnp.exp2(s - m)
        den = jnp.sum(p, axis=-1, keepdims=True)
        o = jnp.dot(p.astype(BF16), vv, preferred_element_type=F32) / den
        return o, jnp.broadcast_to(m + jnp.log2(den), (BLOCK, HEAD_DIM))

    def gather(ref, starts, piece, cols):
        return jnp.concatenate([ref[s:s + piece, cols] for s in starts], axis=0).astype(BF16)

    for g, n_pieces in ((0, MAX_DIL), (1, MAX_DIL // DIL_PAIRS[1][1])):
        piece = BLOCK // n_pieces
        band, band_first = masks(n_pieces)
        q_ref, k_ref, v_ref, kp_ref, vp_ref = ((q0_ref, k0_ref, v0_ref, k0p_ref, v0p_ref) if g == 0 else
                                               (q12_ref, k12_ref, v12_ref, k1p_ref, v1p_ref))
        n_res = MAX_DIL // n_pieces
        for res in range(n_res):
            for h in range(A_HEADS_PER_GROUP):
                cols = slice(h * HEAD_DIM, (h + 1) * HEAD_DIM)
                bases = [(res + n_res * p) * BLOCK for p in range(n_pieces)]
                k_prev = jnp.concatenate([kp_ref[res + n_res * p, :, cols] for p in range(n_pieces)],
                                         axis=0).astype(BF16)
                v_prev = jnp.concatenate([vp_ref[res + n_res * p, :, cols] for p in range(n_pieces)],
                                         axis=0).astype(BF16)
                for a in range(BLOCK // piece):
                    starts = [b0 + a * piece for b0 in bases]
                    k_own = gather(k_ref, starts, piece, cols)
                    v_own = gather(v_ref, starts, piece, cols)
                    o, lse = attend(gather(q_ref, starts, piece, cols),
                                    jnp.concatenate([k_prev, k_own], axis=0),
                                    jnp.concatenate([v_prev, v_own], axis=0),
                                    band_first if a == 0 else band)
                    for p, s0 in enumerate(starts):
                        og_ref[g, s0:s0 + piece, cols] = o[p * piece:(p + 1) * piece]
                        lg_ref[g, s0:s0 + piece, cols] = lse[p * piece:(p + 1) * piece]
                    k_prev, v_prev = k_own, v_own

    band, band_first = masks(1)
    del band
    for c in range(MAX_DIL):
        rows = slice(c * BLOCK, (c + 1) * BLOCK)
        for h in range(A_HEADS_PER_GROUP):
            cols = slice(h * HEAD_DIM, (h + 1) * HEAD_DIM)
            cols2 = slice(A_OUT + h * HEAD_DIM, A_OUT + (h + 1) * HEAD_DIM)
            o2, l2 = attend(q12_ref[rows, cols2],
                            jnp.concatenate([k2p_ref[rows, cols], k12_ref[rows, cols2]], axis=0),
                            jnp.concatenate([v2p_ref[rows, cols], v12_ref[rows, cols2]], axis=0),
                            band_first)
            l0, l1 = lg_ref[0, rows, cols], lg_ref[1, rows, cols]
            mx = jnp.maximum(jnp.maximum(l0, l1), l2)
            e0, e1, e2 = jnp.exp2(l0 - mx), jnp.exp2(l1 - mx), jnp.exp2(l2 - mx)
            acc = e0 * og_ref[0, rows, cols] + e1 * og_ref[1, rows, cols] + e2 * o2
            o_ref[rows, cols] = (acc / (e0 + e1 + e2)).astype(BF16)


def _dilated_attention(q0, k0, v0, q12, k12, v12, batch):
    t = q0.shape[0]
    n_chunks = t // CHUNK
    per_seq = n_chunks // batch
    chunk = lambda b, i: b * per_seq + i
    prev = lambda b, i: b * per_seq + jnp.maximum(i - 1, 0)

    def cur(width):
        return pl.BlockSpec((CHUNK, width), lambda b, i: (chunk(b, i), 0))

    def tail(piece, width, col):
        return pl.BlockSpec((None, MAX_DIL, None, piece, width),
                            lambda b, i: (prev(b, i), 0, BLOCK // piece - 1, 0, col))

    def tail_view(x, piece):
        return x.reshape(n_chunks, MAX_DIL, BLOCK // piece, piece, x.shape[-1])

    piece0 = BLOCK // MAX_DIL
    piece1 = BLOCK // (MAX_DIL // DIL_PAIRS[1][1])
    prev2 = pl.BlockSpec((None, CHUNK, A_OUT), lambda b, i: (prev(b, i), 0, 1))
    view2 = lambda x: x.reshape(n_chunks, CHUNK, A12_WIDTH)
    return pl.pallas_call(
        _dilated_kernel,
        out_shape=jax.ShapeDtypeStruct((t, A_OUT), BF16),
        grid=(batch, per_seq),
        in_specs=[cur(A_OUT), cur(A_OUT), cur(A_OUT), tail(piece0, A_OUT, 0), tail(piece0, A_OUT, 0),
                  cur(A12_WIDTH), cur(A12_WIDTH), cur(A12_WIDTH),
                  tail(piece1, A_OUT, 0), tail(piece1, A_OUT, 0), prev2, prev2],
        out_specs=cur(A_OUT),
        scratch_shapes=[pltpu.VMEM((2, CHUNK, A_OUT), F32), pltpu.VMEM((2, CHUNK, A_OUT), F32)],
        compiler_params=_params(2),
        name="dilated",
    )(q0, k0, v0, tail_view(k0, piece0), tail_view(v0, piece0),
      q12, k12, v12, tail_view(k12, piece1), tail_view(v12, piece1), view2(k12), view2(v12))


def _merge_kernel(h_ref, u_ref, oa_ref, ob_ref, om_ref, unperm_ref,
                  wg_ref, bg_ref, woa_ref, wob_ref, wom_ref, wout_ref, gpost_ref, out_ref):
    for tile in range(STEP_TILES):
        rows = slice(tile * ROW_TILE, (tile + 1) * ROW_TILE)
        u = u_ref[rows, :]

        def gate(idx):
            cols = slice(idx * D_MODEL, (idx + 1) * D_MODEL)
            z = jnp.dot(u, wg_ref[:, cols], preferred_element_type=F32) + bg_ref[:, cols]
            return 1.0 / (1.0 + jnp.exp(-z))

        oa_res = jnp.concatenate([oa_ref[c, tile] for c in range(MAX_DIL)], axis=0)
        o_a = jnp.dot(unperm_ref[...], oa_res, preferred_element_type=F32).astype(BF16)

        merged = gate(0) * jnp.dot(o_a, woa_ref[...], preferred_element_type=F32)
        merged = merged + gate(1) * jnp.dot(ob_ref[rows, :], wob_ref[...], preferred_element_type=F32)
        merged = merged + gate(2) * jnp.dot(om_ref[rows, :], wom_ref[...], preferred_element_type=F32)
        mixed = jnp.dot(merged.astype(BF16), wout_ref[...], preferred_element_type=F32)
        out_ref[rows, :] = h_ref[rows, :] + _rms(mixed, gpost_ref[...])


def _merge(h, u, oa, ob, om, unperm, wg, bg, woa, wob, wom, wout, gpost):
    t = h.shape[0]

    def row(width):
        return pl.BlockSpec((STEP_ROWS, width), lambda i: (i, 0))

    in_specs = ([row(D_MODEL), row(D_MODEL), _res_spec(A_OUT), row(B_WIDTH), row(M_WIDTH),
                 _const_spec((ROW_TILE, ROW_TILE)),
                 _const_spec((D_MODEL, 3 * D_MODEL)), _const_spec((1, 3 * D_MODEL)),
                 _const_spec((A_OUT, D_MODEL)), _const_spec((B_WIDTH, D_MODEL)),
                 _const_spec((M_WIDTH, D_MODEL)), _const_spec((D_MODEL, D_MODEL)),
                 _const_spec((1, D_MODEL))])
    oa = oa.reshape(t // CHUNK, MAX_DIL, TILES_PER_CHUNK, TILE_RUN, A_OUT)
    return pl.pallas_call(
        _merge_kernel,
        out_shape=jax.ShapeDtypeStruct((t, D_MODEL), F32),
        grid=(t // STEP_ROWS,),
        in_specs=in_specs,
        out_specs=row(D_MODEL),
        compiler_params=_params(),
        name="merge",
    )(h, u, oa, ob, om, unperm, wg, bg, woa, wob, wom, wout, gpost)


def _rope_consts(seq):
    half = HEAD_DIM // 2
    inv = ROPE_THETA ** (-np.arange(half, dtype=np.float64) / half)
    inv = np.concatenate([inv, inv])[None, :]
    base = np.arange(0, seq, STEP_ROWS, dtype=np.float64)[:, None] * inv
    local = np.arange(STEP_ROWS, dtype=np.float64)[:, None] * inv
    local_res = (local.reshape(STEP_TILES, TILE_RUN, MAX_DIL, HEAD_DIM).transpose(0, 2, 1, 3)
                 .reshape(STEP_ROWS, HEAD_DIM))
    tables = (np.cos(base), np.sin(base), np.cos(local), np.sin(local), np.cos(local_res), np.sin(local_res))
    return tuple(jnp.asarray(tb, dtype=F32) for tb in tables)


def _residue_perm():
    out_row = np.arange(ROW_TILE)
    src = (out_row % TILE_RUN) * MAX_DIL + out_row // TILE_RUN
    return np.equal(src[:, None], np.arange(ROW_TILE)[None, :]).astype(np.float32)


def kernel(x, mem, ffn1_norm_pre, ffn1_w_in, ffn1_w_out, ffn1_norm_post, mix_norm_pre, w_in, sinks, mem_norm, w_mem_kv, w_gate, b_gate, w_o_a, w_o_b, w_o_m, w_out, mix_norm_post, ffn2_norm_pre, ffn2_w_in, ffn2_w_out, ffn2_norm_post):
    b, s, _ = x.shape
    depth = ffn1_w_in.shape[0]
    t = b * s
    rope_consts = _rope_consts(s)
    perm = jnp.asarray(_residue_perm(), dtype=BF16)
    unperm = jnp.asarray(_residue_perm().T, dtype=BF16)
    h = x.reshape(t, D_MODEL)
    for l in range(depth):
        bf = lambda w: w[l].astype(BF16)
        mk, mv = _mem_kv(mem, mem_norm[l][None], bf(w_mem_kv))
        h1, u = _ffn(h, ffn1_norm_pre[l][None], bf(ffn1_w_in), bf(ffn1_w_out),
                     ffn1_norm_post[l][None], mix_norm_pre[l][None])
        q0, k0, v0, q12, k12, v12, bq, bk, bv, om = _proj(u, bf(w_in), perm, rope_consts, mk, mv, s)

        flat = lambda a: a.reshape(t, a.shape[-1])
        oa = _dilated_attention(flat(q0), flat(k0), flat(v0), flat(q12), flat(k12), flat(v12), b)
        seq3 = lambda a: a.reshape(b, s, a.shape[-1])
        ob = _band_attention(seq3(bq), seq3(bk), seq3(bv), B_WINDOW - 1, sinks[l]).reshape(t, B_WIDTH)

        h2 = _merge(h1, u, oa, ob, om, unperm, bf(w_gate), b_gate[l][None], bf(w_o_a), bf(w_o_b),
                    bf(w_o_m), bf(w_out), mix_norm_post[l][None])
        (h,) = _ffn(h2, ffn2_norm_pre[l][None], bf(ffn2_w_in), bf(ffn2_w_out), ffn2_norm_post[l][None])
    return h.reshape(b, s, D_MODEL)
```

```python
import functools

import jax
import jax.numpy as jnp
import numpy as np
from jax import lax
from jax.experimental import pallas as pl
from jax.experimental.pallas import tpu as pltpu

D_MODEL = 1024
HEAD_DIM = 128
DIL_PAIRS = ((128, 1), (512, 4), (2048, 16))
A_HEADS_PER_GROUP = 2
N_GROUPS = len(DIL_PAIRS)
A_HEADS = A_HEADS_PER_GROUP * N_GROUPS
B_Q_HEADS = 4
B_KV_HEADS = 2
B_WINDOW = 128
M_HEADS = 4
D_FF = 2816
ROPE_THETA = 10000.0
BLOCK = 128
EPS = 1e-6
NEG_INF = -1e30

A_WIDTH = A_HEADS * HEAD_DIM
A_OUT = A_HEADS_PER_GROUP * HEAD_DIM
B_WIDTH = B_Q_HEADS * HEAD_DIM
B_KV_WIDTH = B_KV_HEADS * HEAD_DIM
M_WIDTH = M_HEADS * HEAD_DIM
D_IN = 3 * A_WIDTH + B_WIDTH + 2 * B_KV_WIDTH + M_WIDTH
OFF_AQ = 0
OFF_AK = A_WIDTH
OFF_AV = 2 * A_WIDTH
OFF_BQ = 3 * A_WIDTH
OFF_BK = OFF_BQ + B_WIDTH
OFF_BV = OFF_BK + B_KV_WIDTH
OFF_MQ = OFF_BV + B_KV_WIDTH
QK_SCALE = HEAD_DIM ** -0.5
LOG2E = 1.4426950408889634
Q_SCALE = QK_SCALE * LOG2E

MXU_N = 256
ROW_TILE = 512
STEP_TILES = 2
STEP_ROWS = STEP_TILES * ROW_TILE
FFN_TILE = 256
FFN_TILES = STEP_ROWS // FFN_TILE
MERGE_TILE = 256
MERGE_TILES = STEP_ROWS // MERGE_TILE
ATT_ROWS = 1024
VMEM_LIMIT = 56 * 1024 * 1024

MAX_DIL = max(d for _, d in DIL_PAIRS)
CHUNK = MAX_DIL * BLOCK
TILES_PER_CHUNK = CHUNK // ROW_TILE
STEPS_PER_CHUNK = TILES_PER_CHUNK // STEP_TILES
TILE_RUN = ROW_TILE // MAX_DIL
MERGE_RUN = MERGE_TILE // MAX_DIL
A12_WIDTH = 2 * A_OUT

F32 = jnp.float32
BF16 = jnp.bfloat16
NT_DIMS = (((1,), (1,)), ((), ()))


def _rms(x, g):
    return x * lax.rsqrt(jnp.mean(x * x, axis=-1, keepdims=True) + EPS) * g


def _const_spec(shape):
    nd = len(shape)
    return pl.BlockSpec(shape, lambda *_: (0,) * nd, pipeline_mode=pl.Buffered(1))


def _params(n_axes=1):
    return pltpu.CompilerParams(
        dimension_semantics=("arbitrary",) * n_axes, vmem_limit_bytes=VMEM_LIMIT)


def _mem_kv_kernel(mem_ref, g_ref, w_ref, mk_ref, mv_ref):
    mn = _rms(mem_ref[0], g_ref[...]).astype(BF16)
    kv = jnp.dot(mn, w_ref[...], preferred_element_type=F32)
    mk_ref[0] = kv[:, :M_WIDTH].astype(BF16)
    mv_ref[0] = kv[:, M_WIDTH:].astype(BF16)


def _mem_kv(mem, g, w):
    b, n, _ = mem.shape
    out = jax.ShapeDtypeStruct((b, n, M_WIDTH), BF16)
    return pl.pallas_call(
        _mem_kv_kernel,
        out_shape=(out, out),
        grid=(b,),
        in_specs=[pl.BlockSpec((1, n, D_MODEL), lambda i: (i, 0, 0)),
                  _const_spec((1, D_MODEL)),
                  _const_spec((D_MODEL, 2 * M_WIDTH))],
        out_specs=(pl.BlockSpec((1, n, M_WIDTH), lambda i: (i, 0, 0)),
                   pl.BlockSpec((1, n, M_WIDTH), lambda i: (i, 0, 0))),
        compiler_params=_params(),
        name="mem_kv",
    )(mem, g, w)


def _ffn_kernel(x_ref, gpre_ref, win_ref, wout_ref, gpost_ref, *rest, emit_u):
    if emit_u:
        gnext_ref, h_ref, u_ref, xn_ref, act_ref, f_ref = rest
    else:
        h_ref, xn_ref, act_ref, f_ref = rest
    rows = lambda tile: slice(tile * FFN_TILE, (tile + 1) * FFN_TILE)

    def prologue(tile):
        xn_ref[tile] = _rms(x_ref[rows(tile), :], gpre_ref[...]).astype(BF16)

    def up_chunk(tile, c):
        lo = c * MXU_N
        xn = xn_ref[tile]
        gate = jnp.dot(xn, win_ref[:, lo:lo + MXU_N], preferred_element_type=F32)
        up = jnp.dot(xn, win_ref[:, D_FF + lo:D_FF + lo + MXU_N], preferred_element_type=F32)
        silu = gate * (1.0 / (1.0 + jnp.exp(-gate)))
        act_ref[tile, :, lo:lo + MXU_N] = (silu * up).astype(BF16)

    def epilogue(tile):
        h = x_ref[rows(tile), :] + 0.5 * _rms(f_ref[tile], gpost_ref[...])
        h_ref[rows(tile), :] = h
        if emit_u:
            u_ref[rows(tile), :] = _rms(h, gnext_ref[...]).astype(BF16)

    n_chunks = D_FF // MXU_N
    prologue(0)
    for c in range(n_chunks):
        up_chunk(0, c)
    for tile in range(FFN_TILES):
        if tile + 1 < FFN_TILES:
            prologue(tile + 1)
            up_chunk(tile + 1, 0)
        f_ref[tile] = jnp.dot(act_ref[tile], wout_ref[...], preferred_element_type=F32)
        epilogue(tile)
        if tile + 1 < FFN_TILES:
            for c in range(1, n_chunks):
                up_chunk(tile + 1, c)


def _ffn(x, gpre, w_in, w_out, gpost, gnext=None):
    t = x.shape[0]
    emit_u = gnext is not None
    row = pl.BlockSpec((STEP_ROWS, D_MODEL), lambda i: (i, 0))
    gain = _const_spec((1, D_MODEL))
    in_specs = [row, gain, _const_spec((D_MODEL, 2 * D_FF)), _const_spec((D_FF, D_MODEL)), gain]
    args = [x, gpre, w_in, w_out, gpost]
    out_shape = [jax.ShapeDtypeStruct((t, D_MODEL), F32)]
    out_specs = [row]
    if emit_u:
        in_specs.append(gain)
        args.append(gnext)
        out_shape.append(jax.ShapeDtypeStruct((t, D_MODEL), BF16))
        out_specs.append(row)
    return pl.pallas_call(
        functools.partial(_ffn_kernel, emit_u=emit_u),
        out_shape=tuple(out_shape),
        grid=(t // STEP_ROWS,),
        in_specs=in_specs,
        out_specs=tuple(out_specs),
        scratch_shapes=[pltpu.VMEM((FFN_TILES, FFN_TILE, D_MODEL), BF16),
                        pltpu.VMEM((FFN_TILES, FFN_TILE, D_FF), BF16),
                        pltpu.VMEM((FFN_TILES, FFN_TILE, D_MODEL), F32)],
        compiler_params=_params(),
        name="ffn_u" if emit_u else "ffn",
    )(*args)


def _proj_kernel(u_ref, w_ref, perm_ref, cb_ref, sb_ref, cl_ref, sl_ref, clr_ref, slr_ref, mk_ref, mv_ref,
                 q0_ref, k0_ref, v0_ref, q12_ref, k12_ref, v12_ref, bq_ref, bk_ref, bv_ref, om_ref,
                 *, steps_per_seq):
    step = pl.program_id(0) % steps_per_seq
    cb, sb = cb_ref[pl.ds(step, 1), :], sb_ref[pl.ds(step, 1), :]
    lane = lax.broadcasted_iota(jnp.int32, (1, HEAD_DIM), 1)
    sign = jnp.where(lane < HEAD_DIM // 2, -1.0, 1.0)

    def rope_tables(cl, sl, scale):
        cos = (cb * scale) * cl - (sb * scale) * sl
        sin = (sb * (sign * scale)) * cl + (cb * (sign * scale)) * sl
        return cos, sin

    for tile in range(STEP_TILES):
        rows = slice(tile * ROW_TILE, (tile + 1) * ROW_TILE)
        cl, sl, clr, slr = cl_ref[rows, :], sl_ref[rows, :], clr_ref[rows, :], slr_ref[rows, :]
        _proj_tile(tile, u_ref, w_ref, perm_ref,
                   rope_tables(cl, sl, Q_SCALE) + rope_tables(cl, sl, 1.0),
                   rope_tables(clr, slr, Q_SCALE) + rope_tables(clr, slr, 1.0), mk_ref, mv_ref,
                   q0_ref, k0_ref, v0_ref, q12_ref, k12_ref, v12_ref, bq_ref, bk_ref, bv_ref, om_ref)


def _proj_tile(tile, u_ref, w_ref, perm_ref, tables, tables_res, mk_ref, mv_ref,
               q0_ref, k0_ref, v0_ref, q12_ref, k12_ref, v12_ref, bq_ref, bk_ref, bv_ref, om_ref):
    rows = slice(tile * ROW_TILE, (tile + 1) * ROW_TILE)
    u = u_ref[rows, :]
    u_res = jnp.dot(perm_ref[...], u, preferred_element_type=F32).astype(BF16)

    def proj(lhs, lo):
        return jnp.dot(lhs, w_ref[:, lo:lo + MXU_N], preferred_element_type=F32)

    def rope(y, cos, sin):
        out = []
        for hh in range(MXU_N // HEAD_DIM):
            yh = y[:, hh * HEAD_DIM:(hh + 1) * HEAD_DIM]
            out.append(yh * cos + pltpu.roll(yh, HEAD_DIM // 2, 1) * sin)
        return jnp.concatenate(out, axis=1)

    def put_res(dst, col, y):
        for c in range(MAX_DIL):
            dst[c, tile, :, col:col + MXU_N] = y[c * TILE_RUN:(c + 1) * TILE_RUN].astype(dst.dtype)

    cqp, sqp, ckp, skp = tables_res
    for g in range(N_GROUPS):
        dst_q, dst_k, dst_v = (q0_ref, k0_ref, v0_ref) if g == 0 else (q12_ref, k12_ref, v12_ref)
        col = 0 if g == 0 else (g - 1) * A_OUT
        put_res(dst_q, col, rope(proj(u_res, OFF_AQ + g * A_OUT), cqp, sqp))
        put_res(dst_k, col, rope(proj(u_res, OFF_AK + g * A_OUT), ckp, skp))
        put_res(dst_v, col, proj(u_res, OFF_AV + g * A_OUT))

    cq, sq, ck, sk = tables
    for c in range(B_WIDTH // MXU_N):
        bq_ref[rows, c * MXU_N:(c + 1) * MXU_N] = rope(proj(u, OFF_BQ + c * MXU_N), cq, sq).astype(BF16)
    bk_ref[rows, :] = rope(proj(u, OFF_BK), ck, sk).astype(BF16)
    bv_ref[rows, :] = proj(u, OFF_BV).astype(BF16)

    for c in range(M_WIDTH // MXU_N):
        y = proj(u, OFF_MQ + c * MXU_N) * Q_SCALE
        for hh in range(MXU_N // HEAD_DIM):
            h = c * (MXU_N // HEAD_DIM) + hh
            cols = slice(h * HEAD_DIM, (h + 1) * HEAD_DIM)
            q = y[:, hh * HEAD_DIM:(hh + 1) * HEAD_DIM].astype(BF16)
            s = lax.dot_general(q, mk_ref[0, :, cols], NT_DIMS, preferred_element_type=F32)
            m = jnp.max(s, axis=-1, keepdims=True)
            p = jnp.exp2(s - m)
            den = jnp.sum(p, axis=-1, keepdims=True)
            o = jnp.dot(p.astype(BF16), mv_ref[0, :, cols], preferred_element_type=F32)
            om_ref[rows, cols] = (o / den).astype(BF16)


def _res_spec(width):
    return pl.BlockSpec((None, MAX_DIL, STEP_TILES, TILE_RUN, width),
                        lambda i: (i // STEPS_PER_CHUNK, 0, i % STEPS_PER_CHUNK, 0, 0))


def _proj(u, w_in, perm, rope_consts, mk, mv, seq):
    t = u.shape[0]
    n_mem = mk.shape[1]
    tiles_per_seq = seq // STEP_ROWS

    def row(width):
        return pl.BlockSpec((STEP_ROWS, width), lambda i: (i, 0))

    def res_shape(width, dtype):
        return jax.ShapeDtypeStruct((t // CHUNK, MAX_DIL, TILES_PER_CHUNK, TILE_RUN, width), dtype)

    base_table = _const_spec((tiles_per_seq, HEAD_DIM))
    local_table = _const_spec((STEP_ROWS, HEAD_DIM))
    memb =pl.BlockSpec((1, n_mem, M_WIDTH), lambda i: (i // tiles_per_seq, 0, 0))
    out_shape = ([res_shape(A_OUT, F32)] * 3 + [res_shape(A12_WIDTH, BF16)] * 3
                 + [jax.ShapeDtypeStruct((t, w), BF16) for w in (B_WIDTH, B_KV_WIDTH, B_KV_WIDTH, M_WIDTH)])
    out_specs = ([_res_spec(A_OUT)] * 3 + [_res_spec(A12_WIDTH)] * 3
                 + [row(w) for w in (B_WIDTH, B_KV_WIDTH, B_KV_WIDTH, M_WIDTH)])
    return pl.pallas_call(
        functools.partial(_proj_kernel, steps_per_seq=tiles_per_seq),
        out_shape=tuple(out_shape),
        grid=(t // STEP_ROWS,),
        in_specs=[row(D_MODEL), _const_spec((D_MODEL, D_IN)), _const_spec((ROW_TILE, ROW_TILE))]
        + [base_table] * 2 + [local_table] * 4 + [memb, memb],
        out_specs=tuple(out_specs),
        compiler_params=_params(),
        name="proj",
    )(u, w_in, perm, *rope_consts, mk, mv)


def _band_kernel(sink_ref, q_ref, k_ref, v_ref, kp_ref, vp_ref, o_ref, *, hq, hkv, max_dist, n_blk):
    grp = hq // hkv
    first_tile = pl.program_id(1) == 0

    row = lax.broadcasted_iota(jnp.int32, (BLOCK, 2 * BLOCK), 0)
    col = lax.broadcasted_iota(jnp.int32, (BLOCK, 2 * BLOCK), 1)
    dist = row + BLOCK - col
    band = (dist >= 0) & (dist <= max_dist)
    band_first = band & ((col >= BLOCK) | jnp.logical_not(first_tile))

    for j in range(n_blk):
        rows = slice(j * BLOCK, (j + 1) * BLOCK)
        mask = band_first if j == 0 else band
        for hk in range(hkv):
            kcols = slice(hk * HEAD_DIM, (hk + 1) * HEAD_DIM)
            if j == 0:
                kk = jnp.concatenate([kp_ref[0, :, kcols], k_ref[0, rows, kcols]], axis=0)
                vv = jnp.concatenate([vp_ref[0, :, kcols], v_ref[0, rows, kcols]], axis=0)
            else:
                both = slice((j - 1) * BLOCK, (j + 1) * BLOCK)
                kk = k_ref[0, both, kcols]
                vv = v_ref[0, both, kcols]
            for g in range(grp):
                h = hk * grp + g
                qcols = slice(h * HEAD_DIM, (h + 1) * HEAD_DIM)
                s = lax.dot_general(q_ref[0, rows, qcols], kk, NT_DIMS, preferred_element_type=F32)
                s = jnp.where(mask, s, NEG_INF)
                sk = sink_ref[h] * LOG2E
                m = jnp.maximum(jnp.max(s, axis=-1, keepdims=True), sk)
                p = jnp.exp2(s - m)
                tot = jnp.sum(p, axis=-1, keepdims=True) + jnp.exp2(sk - m)
                o = jnp.dot(p.astype(BF16), vv, preferred_element_type=F32)
                o_ref[0, rows, qcols] = (o / tot).astype(BF16)


def _band_attention(q, k, v, max_dist, sink):
    n, length, qw = q.shape
    kw = k.shape[2]
    rows = min(ATT_ROWS, length)
    n_blk = rows // BLOCK
    cur = lambda w: pl.BlockSpec((1, rows, w), lambda b, i: (b, i, 0))
    prev = pl.BlockSpec((1, BLOCK, kw), lambda b, i: (b, jnp.maximum(i * n_blk - 1, 0), 0))
    return pl.pallas_call(
        functools.partial(_band_kernel, hq=qw // HEAD_DIM, hkv=kw // HEAD_DIM, max_dist=max_dist,
                          n_blk=n_blk),
        out_shape=jax.ShapeDtypeStruct((n, length, qw), BF16),
        grid=(n, length // rows),
        in_specs=[pl.BlockSpec(memory_space=pltpu.SMEM), cur(qw), cur(kw), cur(kw), prev, prev],
        out_specs=cur(qw),
        compiler_params=_params(2),
        name="band_sink",
    )(sink, q, k, v, k, v)


def _dilated_kernel(q0_ref, k0_ref, v0_ref, k0p_ref, v0p_ref,
                    q12_ref, k12_ref, v12_ref, k1p_ref, v1p_ref, k2p_ref, v2p_ref,
                    o_ref, og_ref, lg_ref):
    first_chunk = pl.program_id(1) == 0
    row = lax.broadcasted_iota(jnp.int32, (BLOCK, 2 * BLOCK), 0)
    col = lax.broadcasted_iota(jnp.int32, (BLOCK, 2 * BLOCK), 1)
    own = col >= BLOCK
    kcol = jnp.where(own, col - BLOCK, col)

    def masks(n_pieces):
        piece = BLOCK // n_pieces
        shift = piece.bit_length() - 1
        pos = lambda r: (r & (piece - 1)) * n_pieces + (r >> shift)
        dist = pos(row) - pos(kcol) + jnp.where(own, 0, BLOCK)
        band = (dist >= 0) & (dist <= BLOCK)
        return band, band & (own | jnp.logical_not(first_chunk))

    def attend(q, kk, vv, mask):
        s = lax.dot_general(q, kk, NT_DIMS, preferred_element_type=F32)
        s = jnp.where(mask, s, NEG_INF)
        m = jnp.max(s, axis=-1, keepdims=True)
        p = jnp.exp2(s - m)
        den = jnp.sum(p, axis=-1, keepdims=True)
        o = jnp.dot(p.astype(BF16), vv, preferred_element_type=F32) / den
        return o, jnp.broadcast_to(m + jnp.log2(den), (BLOCK, HEAD_DIM))

    def gather(ref, starts, piece, cols):
        return jnp.concatenate([ref[s:s + piece, cols] for s in starts], axis=0).astype(BF16)

    for g, n_pieces in ((0, MAX_DIL), (1, MAX_DIL // DIL_PAIRS[1][1])):
        piece = BLOCK // n_pieces
        band, band_first = masks(n_pieces)
        q_ref, k_ref, v_ref, kp_ref, vp_ref = ((q0_ref, k0_ref, v0_ref, k0p_ref, v0p_ref) if g == 0 else
                                               (q12_ref, k12_ref, v12_ref, k1p_ref, v1p_ref))
        n_res = MAX_DIL // n_pieces
        for res in range(n_res):
            for h in range(A_HEADS_PER_GROUP):
                cols = slice(h * HEAD_DIM, (h + 1) * HEAD_DIM)
                bases = [(res + n_res * p) * BLOCK for p in range(n_pieces)]
                k_prev = jnp.concatenate([kp_ref[res + n_res * p, :, cols] for p in range(n_pieces)],
                                         axis=0).astype(BF16)
                v_prev = jnp.concatenate([vp_ref[res + n_res * p, :, cols] for p in range(n_pieces)],
                                         axis=0).astype(BF16)
                for a in range(BLOCK // piece):
                    starts = [b0 + a * piece for b0 in bases]
                    k_own = gather(k_ref, starts, piece, cols)
                    v_own = gather(v_ref, starts, piece, cols)
                    o, lse = attend(gather(q_ref, starts, piece, cols),
                                    jnp.concatenate([k_prev, k_own], axis=0),
                                    jnp.concatenate([v_prev, v_own], axis=0),
                                    band_first if a == 0 else band)
                    for p, s0 in enumerate(starts):
                        og_ref[g, s0:s0 + piece, cols] = o[p * piece:(p + 1) * piece]
                        lg_ref[g, s0:s0 + piece, cols] = lse[p * piece:(p + 1) * piece]
                    k_prev, v_prev = k_own, v_own

    band, band_first = masks(1)
    del band
    for c in range(MAX_DIL):
        rows = slice(c * BLOCK, (c + 1) * BLOCK)
        for h in range(A_HEADS_PER_GROUP):
            cols = slice(h * HEAD_DIM, (h + 1) * HEAD_DIM)
            cols2 = slice(A_OUT + h * HEAD_DIM, A_OUT + (h + 1) * HEAD_DIM)
            o2, l2 = attend(q12_ref[rows, cols2],
                            jnp.concatenate([k2p_ref[rows, cols], k12_ref[rows, cols2]], axis=0),
                            jnp.concatenate([v2p_ref[rows, cols], v12_ref[rows, cols2]], axis=0),
                            band_first)
            l0, l1 = lg_ref[0, rows, cols], lg_ref[1, rows, cols]
            mx = jnp.maximum(jnp.maximum(l0, l1), l2)
            e0, e1, e2 = jnp.exp2(l0 - mx), jnp.exp2(l1 - mx), jnp.exp2(l2 - mx)
            acc = e0 * og_ref[0, rows, cols] + e1 * og_ref[1, rows, cols] + e2 * o2
            o_ref[rows, cols] = (acc / (e0 + e1 + e2)).astype(BF16)


def _dilated_attention(q0, k0, v0, q12, k12, v12, batch):
    t = q0.shape[0]
    n_chunks = t // CHUNK
    per_seq = n_chunks // batch
    chunk = lambda b, i: b * per_seq + i
    prev = lambda b, i: b * per_seq + jnp.maximum(i - 1, 0)

    def cur(width):
        return pl.BlockSpec((CHUNK, width), lambda b, i: (chunk(b, i), 0))

    def tail(piece, width, col):
        return pl.BlockSpec((None, MAX_DIL, None, piece, width),
                            lambda b, i: (prev(b, i), 0, BLOCK // piece - 1, 0, col))

    def tail_view(x, piece):
        return x.reshape(n_chunks, MAX_DIL, BLOCK // piece, piece, x.shape[-1])

    piece0 = BLOCK // MAX_DIL
    piece1 = BLOCK // (MAX_DIL // DIL_PAIRS[1][1])
    prev2 = pl.BlockSpec((None, CHUNK, A_OUT), lambda b, i: (prev(b, i), 0, 1))
    view2 = lambda x: x.reshape(n_chunks, CHUNK, A12_WIDTH)
    return pl.pallas_call(
        _dilated_kernel,
        out_shape=jax.ShapeDtypeStruct((t, A_OUT), BF16),
        grid=(batch, per_seq),
        in_specs=[cur(A_OUT), cur(A_OUT), cur(A_OUT), tail(piece0, A_OUT, 0), tail(piece0, A_OUT, 0),
                  cur(A12_WIDTH), cur(A12_WIDTH), cur(A12_WIDTH),
                  tail(piece1, A_OUT, 0), tail(piece1, A_OUT, 0), prev2, prev2],
        out_specs=cur(A_OUT),
        scratch_shapes=[pltpu.VMEM((2, CHUNK, A_OUT), F32), pltpu.VMEM((2, CHUNK, A_OUT), F32)],
        compiler_params=_params(2),
        name="dilated",
    )(q0, k0, v0, tail_view(k0, piece0), tail_view(v0, piece0),
      q12, k12, v12, tail_view(k12, piece1), tail_view(v12, piece1), view2(k12), view2(v12))


def _merge_kernel(h_ref, u_ref, oa_ref, ob_ref, om_ref, unperm_ref,
                  wg_ref, bg_ref, woa_ref, wob_ref, wom_ref, wout_ref, gpost_ref, out_ref):
    for tile in range(MERGE_TILES):
        rows = slice(tile * MERGE_TILE, (tile + 1) * MERGE_TILE)
        u = u_ref[rows, :]

        def gate(idx):
            cols = slice(idx * D_MODEL, (idx + 1) * D_MODEL)
            z = jnp.dot(u, wg_ref[:, cols], preferred_element_type=F32) + bg_ref[:, cols]
            return 1.0 / (1.0 + jnp.exp(-z))

        oa_res = jnp.concatenate([oa_ref[c, tile] for c in range(MAX_DIL)], axis=0)
        o_a = jnp.dot(unperm_ref[...], oa_res, preferred_element_type=F32).astype(BF16)

        merged = gate(0) * jnp.dot(o_a, woa_ref[...], preferred_element_type=F32)
        merged = merged + gate(1) * jnp.dot(ob_ref[rows, :], wob_ref[...], preferred_element_type=F32)
        merged = merged + gate(2) * jnp.dot(om_ref[rows, :], wom_ref[...], preferred_element_type=F32)
        mixed = jnp.dot(merged.astype(BF16), wout_ref[...], preferred_element_type=F32)
        out_ref[rows, :] = h_ref[rows, :] + _rms(mixed, gpost_ref[...])


def _merge(h, u, oa, ob, om, unperm, wg, bg, woa, wob, wom, wout, gpost):
    t = h.shape[0]

    def row(width):
        return pl.BlockSpec((STEP_ROWS, width), lambda i: (i, 0))

    oa_spec = pl.BlockSpec((None, MAX_DIL, MERGE_TILES, MERGE_RUN, A_OUT),
                           lambda i: (i // STEPS_PER_CHUNK, 0, i % STEPS_PER_CHUNK, 0, 0))
    in_specs = ([row(D_MODEL), row(D_MODEL), oa_spec, row(B_WIDTH), row(M_WIDTH),
                 _const_spec((MERGE_TILE, MERGE_TILE)),
                 _const_spec((D_MODEL, 3 * D_MODEL)), _const_spec((1, 3 * D_MODEL)),
                 _const_spec((A_OUT, D_MODEL)), _const_spec((B_WIDTH, D_MODEL)),
                 _const_spec((M_WIDTH, D_MODEL)), _const_spec((D_MODEL, D_MODEL)),
                 _const_spec((1, D_MODEL))])
    oa = oa.reshape(t // CHUNK, MAX_DIL, CHUNK // MERGE_TILE, MERGE_RUN, A_OUT)
    return pl.pallas_call(
        _merge_kernel,
        out_shape=jax.ShapeDtypeStruct((t, D_MODEL), F32),
        grid=(t // STEP_ROWS,),
        in_specs=in_specs,
        out_specs=row(D_MODEL),
        compiler_params=_params(),
        name="merge",
    )(h, u, oa, ob, om, unperm, wg, bg, woa, wob, wom, wout, gpost)


def _rope_consts(seq):
    half = HEAD_DIM // 2
    inv = ROPE_THETA ** (-np.arange(half, dtype=np.float64) / half)
    inv = np.concatenate([inv, inv])[None, :]
    base = np.arange(0, seq, STEP_ROWS, dtype=np.float64)[:, None] * inv
    local = np.arange(STEP_ROWS, dtype=np.float64)[:, None] * inv
    local_res = (local.reshape(STEP_TILES, TILE_RUN, MAX_DIL, HEAD_DIM).transpose(0, 2, 1, 3)
                 .reshape(STEP_ROWS, HEAD_DIM))
    tables = (np.cos(base), np.sin(base), np.cos(local), np.sin(local), np.cos(local_res), np.sin(local_res))
    return tuple(jnp.asarray(tb, dtype=F32) for tb in tables)


def _residue_perm(tile):
    run = tile // MAX_DIL
    out_row = np.arange(tile)
    src = (out_row % run) * MAX_DIL + out_row // run
    return np.equal(src[:, None], np.arange(tile)[None, :]).astype(np.float32)


def kernel(x, mem, ffn1_norm_pre, ffn1_w_in, ffn1_w_out, ffn1_norm_post, mix_norm_pre, w_in, sinks, mem_norm, w_mem_kv, w_gate, b_gate, w_o_a, w_o_b, w_o_m, w_out, mix_norm_post, ffn2_norm_pre, ffn2_w_in, ffn2_w_out, ffn2_norm_post):
    b, s, _ = x.shape
    depth = ffn1_w_in.shape[0]
    t = b * s
    rope_consts = _rope_consts(s)
    perm = jnp.asarray(_residue_perm(ROW_TILE), dtype=BF16)
    unperm = jnp.asarray(_residue_perm(MERGE_TILE).T, dtype=BF16)
    h = x.reshape(t, D_MODEL)
    for l in range(depth):
        bf = lambda w: w[l].astype(BF16)
        mk, mv = _mem_kv(mem, mem_norm[l][None], bf(w_mem_kv))
        h1, u = _ffn(h, ffn1_norm_pre[l][None], bf(ffn1_w_in), bf(ffn1_w_out),
                     ffn1_norm_post[l][None], mix_norm_pre[l][None])
        q0, k0, v0, q12, k12, v12, bq, bk, bv, om = _proj(u, bf(w_in), perm, rope_consts, mk, mv, s)

        flat = lambda a: a.reshape(t, a.shape[-1])
        oa = _dilated_attention(flat(q0), flat(k0), flat(v0), flat(q12), flat(k12), flat(v12), b)
        seq3 = lambda a: a.reshape(b, s, a.shape[-1])
        ob = _band_attention(seq3(bq), seq3(bk), seq3(bv), B_WINDOW - 1, sinks[l]).reshape(t, B_WIDTH)

        h2 = _merge(h1, u, oa, ob, om, unperm, bf(w_gate), b_gate[l][None], bf(w_o_a), bf(w_o_b),
                    bf(w_o_m), bf(w_out), mix_norm_post[l][None])
        (h,) = _ffn(h2, ffn2_norm_pre[l][None], bf(ffn2_w_in), bf(ffn2_w_out), ffn2_norm_post[l][None])
    return h.reshape(b, s, D_MODEL)
```

```python
import functools

import jax
import jax.numpy as jnp
import numpy as np
from jax import lax
from jax.experimental import pallas as pl
from jax.experimental.pallas import tpu as pltpu

D_MODEL = 1024
HEAD_DIM = 128
DIL_PAIRS = ((128, 1), (512, 4), (2048, 16))
A_HEADS_PER_GROUP = 2
N_GROUPS = len(DIL_PAIRS)
A_HEADS = A_HEADS_PER_GROUP * N_GROUPS
B_Q_HEADS = 4
B_KV_HEADS = 2
B_WINDOW = 128
M_HEADS = 4
D_FF = 2816
ROPE_THETA = 10000.0
BLOCK = 128
EPS = 1e-6
NEG_INF = -1e30

A_WIDTH = A_HEADS * HEAD_DIM
A_OUT = A_HEADS_PER_GROUP * HEAD_DIM
B_WIDTH = B_Q_HEADS * HEAD_DIM
B_KV_WIDTH = B_KV_HEADS * HEAD_DIM
M_WIDTH = M_HEADS * HEAD_DIM
D_IN = 3 * A_WIDTH + B_WIDTH + 2 * B_KV_WIDTH + M_WIDTH
OFF_AQ = 0
OFF_AK = A_WIDTH
OFF_AV = 2 * A_WIDTH
OFF_BQ = 3 * A_WIDTH
OFF_BK = OFF_BQ + B_WIDTH
OFF_BV = OFF_BK + B_KV_WIDTH
OFF_MQ = OFF_BV + B_KV_WIDTH
QK_SCALE = HEAD_DIM ** -0.5
LOG2E = 1.4426950408889634
Q_SCALE = QK_SCALE * LOG2E

MXU_N = 256
ROW_TILE = 512
STEP_TILES = 2
STEP_ROWS = STEP_TILES * ROW_TILE
FFN_TILE = 256
FFN_TILES = STEP_ROWS // FFN_TILE
MERGE_TILE = 256
MERGE_TILES = STEP_ROWS // MERGE_TILE
ATT_ROWS = 2048
VMEM_LIMIT = 56 * 1024 * 1024

MAX_DIL = max(d for _, d in DIL_PAIRS)
CHUNK = MAX_DIL * BLOCK
TILES_PER_CHUNK = CHUNK // ROW_TILE
STEPS_PER_CHUNK = TILES_PER_CHUNK // STEP_TILES
TILE_RUN = ROW_TILE // MAX_DIL
MERGE_RUN = MERGE_TILE // MAX_DIL
A12_WIDTH = 2 * A_OUT

F32 = jnp.float32
BF16 = jnp.bfloat16
NT_DIMS = (((1,), (1,)), ((), ()))


def _rms(x, g):
    return x * lax.rsqrt(jnp.mean(x * x, axis=-1, keepdims=True) + EPS) * g


def _const_spec(shape):
    nd = len(shape)
    return pl.BlockSpec(shape, lambda *_: (0,) * nd, pipeline_mode=pl.Buffered(1))


def _params(n_axes=1):
    return pltpu.CompilerParams(
        dimension_semantics=("arbitrary",) * n_axes, vmem_limit_bytes=VMEM_LIMIT)


def _mem_kv_kernel(mem_ref, g_ref, w_ref, mk_ref, mv_ref):
    mn = _rms(mem_ref[0], g_ref[...]).astype(BF16)
    kv = jnp.dot(mn, w_ref[...], preferred_element_type=F32)
    mk_ref[0] = kv[:, :M_WIDTH].astype(BF16)
    mv_ref[0] = kv[:, M_WIDTH:].astype(BF16)


def _mem_kv(mem, g, w):
    b, n, _ = mem.shape
    out = jax.ShapeDtypeStruct((b, n, M_WIDTH), BF16)
    return pl.pallas_call(
        _mem_kv_kernel,
        out_shape=(out, out),
        grid=(b,),
        in_specs=[pl.BlockSpec((1, n, D_MODEL), lambda i: (i, 0, 0)),
                  _const_spec((1, D_MODEL)),
                  _const_spec((D_MODEL, 2 * M_WIDTH))],
        out_specs=(pl.BlockSpec((1, n, M_WIDTH), lambda i: (i, 0, 0)),
                   pl.BlockSpec((1, n, M_WIDTH), lambda i: (i, 0, 0))),
        compiler_params=_params(),
        name="mem_kv",
    )(mem, g, w)


def _ffn_kernel(x_ref, gpre_ref, win_ref, wout_ref, gpost_ref, *rest, emit_u):
    if emit_u:
        gnext_ref, h_ref, u_ref, xn_ref, act_ref, f_ref = rest
    else:
        h_ref, xn_ref, act_ref, f_ref = rest
    rows = lambda tile: slice(tile * FFN_TILE, (tile + 1) * FFN_TILE)

    def prologue(tile):
        xn_ref[tile] = _rms(x_ref[rows(tile), :], gpre_ref[...]).astype(BF16)

    def up_chunk(tile, c):
        lo = c * MXU_N
        xn = xn_ref[tile]
        gate = jnp.dot(xn, win_ref[:, lo:lo + MXU_N], preferred_element_type=F32)
        up = jnp.dot(xn, win_ref[:, D_FF + lo:D_FF + lo + MXU_N], preferred_element_type=F32)
        silu = gate * (1.0 / (1.0 + jnp.exp(-gate)))
        act_ref[tile, :, lo:lo + MXU_N] = (silu * up).astype(BF16)

    def epilogue(tile):
        h = x_ref[rows(tile), :] + 0.5 * _rms(f_ref[tile], gpost_ref[...])
        h_ref[rows(tile), :] = h
        if emit_u:
            u_ref[rows(tile), :] = _rms(h, gnext_ref[...]).astype(BF16)

    n_chunks = D_FF // MXU_N
    prologue(0)
    for c in range(n_chunks):
        up_chunk(0, c)
    for tile in range(FFN_TILES):
        if tile + 1 < FFN_TILES:
            prologue(tile + 1)
            up_chunk(tile + 1, 0)
        f_ref[tile] = jnp.dot(act_ref[tile], wout_ref[...], preferred_element_type=F32)
        epilogue(tile)
        if tile + 1 < FFN_TILES:
            for c in range(1, n_chunks):
                up_chunk(tile + 1, c)


def _ffn(x, gpre, w_in, w_out, gpost, gnext=None):
    t = x.shape[0]
    emit_u = gnext is not None
    row = pl.BlockSpec((STEP_ROWS, D_MODEL), lambda i: (i, 0))
    gain = _const_spec((1, D_MODEL))
    in_specs = [row, gain, _const_spec((D_MODEL, 2 * D_FF)), _const_spec((D_FF, D_MODEL)), gain]
    args = [x, gpre, w_in, w_out, gpost]
    out_shape = [jax.ShapeDtypeStruct((t, D_MODEL), F32)]
    out_specs = [row]
    if emit_u:
        in_specs.append(gain)
        args.append(gnext)
        out_shape.append(jax.ShapeDtypeStruct((t, D_MODEL), BF16))
        out_specs.append(row)
    return pl.pallas_call(
        functools.partial(_ffn_kernel, emit_u=emit_u),
        out_shape=tuple(out_shape),
        grid=(t // STEP_ROWS,),
        in_specs=in_specs,
        out_specs=tuple(out_specs),
        scratch_shapes=[pltpu.VMEM((FFN_TILES, FFN_TILE, D_MODEL), BF16),
                        pltpu.VMEM((FFN_TILES, FFN_TILE, D_FF), BF16),
                        pltpu.VMEM((FFN_TILES, FFN_TILE, D_MODEL), F32)],
        compiler_params=_params(),
        name="ffn_u" if emit_u else "ffn",
    )(*args)


def _proj_kernel(u_ref, w_ref, perm_ref, cb_ref, sb_ref, cl_ref, sl_ref, clr_ref, slr_ref, mk_ref, mv_ref,
                 q0_ref, k0_ref, v0_ref, q12_ref, k12_ref, v12_ref, bq_ref, bk_ref, bv_ref, om_ref,
                 *, steps_per_seq):
    step = pl.program_id(0) % steps_per_seq
    cb, sb = cb_ref[pl.ds(step, 1), :], sb_ref[pl.ds(step, 1), :]
    lane = lax.broadcasted_iota(jnp.int32, (1, HEAD_DIM), 1)
    sign = jnp.where(lane < HEAD_DIM // 2, -1.0, 1.0)

    def rope_tables(cl, sl, scale):
        cos = (cb * scale) * cl - (sb * scale) * sl
        sin = (sb * (sign * scale)) * cl + (cb * (sign * scale)) * sl
        return cos, sin

    for tile in range(STEP_TILES):
        rows = slice(tile * ROW_TILE, (tile + 1) * ROW_TILE)
        cl, sl, clr, slr = cl_ref[rows, :], sl_ref[rows, :], clr_ref[rows, :], slr_ref[rows, :]
        _proj_tile(tile, u_ref, w_ref, perm_ref,
                   rope_tables(cl, sl, Q_SCALE) + rope_tables(cl, sl, 1.0),
                   rope_tables(clr, slr, Q_SCALE) + rope_tables(clr, slr, 1.0), mk_ref, mv_ref,
                   q0_ref, k0_ref, v0_ref, q12_ref, k12_ref, v12_ref, bq_ref, bk_ref, bv_ref, om_ref)


def _proj_tile(tile, u_ref, w_ref, perm_ref, tables, tables_res, mk_ref, mv_ref,
               q0_ref, k0_ref, v0_ref, q12_ref, k12_ref, v12_ref, bq_ref, bk_ref, bv_ref, om_ref):
    rows = slice(tile * ROW_TILE, (tile + 1) * ROW_TILE)
    u = u_ref[rows, :]
    u_res = jnp.dot(perm_ref[...], u, preferred_element_type=F32).astype(BF16)

    def proj(lhs, lo):
        return jnp.dot(lhs, w_ref[:, lo:lo + MXU_N], preferred_element_type=F32)

    def rope(y, cos, sin):
        out = []
        for hh in range(MXU_N // HEAD_DIM):
            yh = y[:, hh * HEAD_DIM:(hh + 1) * HEAD_DIM]
            out.append(yh * cos + pltpu.roll(yh, HEAD_DIM // 2, 1) * sin)
        return jnp.concatenate(out, axis=1)

    def put_res(dst, col, y):
        for c in range(MAX_DIL):
            dst[c, tile, :, col:col + MXU_N] = y[c * TILE_RUN:(c + 1) * TILE_RUN].astype(dst.dtype)

    cqp, sqp, ckp, skp = tables_res
    for g in range(N_GROUPS):
        dst_q, dst_k, dst_v = (q0_ref, k0_ref, v0_ref) if g == 0 else (q12_ref, k12_ref, v12_ref)
        col = 0 if g == 0 else (g - 1) * A_OUT
        put_res(dst_q, col, rope(proj(u_res, OFF_AQ + g * A_OUT), cqp, sqp))
        put_res(dst_k, col, rope(proj(u_res, OFF_AK + g * A_OUT), ckp, skp))
        put_res(dst_v, col, proj(u_res, OFF_AV + g * A_OUT))

    cq, sq, ck, sk = tables
    for c in range(B_WIDTH // MXU_N):
        bq_ref[rows, c * MXU_N:(c + 1) * MXU_N] = rope(proj(u, OFF_BQ + c * MXU_N), cq, sq).astype(BF16)
    bk_ref[rows, :] = rope(proj(u, OFF_BK), ck, sk).astype(BF16)
    bv_ref[rows, :] = proj(u, OFF_BV).astype(BF16)

    for c in range(M_WIDTH // MXU_N):
        y = proj(u, OFF_MQ + c * MXU_N) * Q_SCALE
        for hh in range(MXU_N // HEAD_DIM):
            h = c * (MXU_N // HEAD_DIM) + hh
            cols = slice(h * HEAD_DIM, (h + 1) * HEAD_DIM)
            q = y[:, hh * HEAD_DIM:(hh + 1) * HEAD_DIM].astype(BF16)
            s = lax.dot_general(q, mk_ref[0, :, cols], NT_DIMS, preferred_element_type=F32)
            m = jnp.max(s, axis=-1, keepdims=True)
            p = jnp.exp2(s - m)
            den = jnp.sum(p, axis=-1, keepdims=True)
            o = jnp.dot(p.astype(BF16), mv_ref[0, :, cols], preferred_element_type=F32)
            om_ref[rows, cols] = (o / den).astype(BF16)


def _res_spec(width):
    return pl.BlockSpec((None, MAX_DIL, STEP_TILES, TILE_RUN, width),
                        lambda i: (i // STEPS_PER_CHUNK, 0, i % STEPS_PER_CHUNK, 0, 0))


def _proj(u, w_in, perm, rope_consts, mk, mv, seq):
    t = u.shape[0]
    n_mem = mk.shape[1]
    tiles_per_seq = seq // STEP_ROWS

    def row(width):
        return pl.BlockSpec((STEP_ROWS, width), lambda i: (i, 0))

    def res_shape(width, dtype):
        return jax.ShapeDtypeStruct((t // CHUNK, MAX_DIL, TILES_PER_CHUNK, TILE_RUN, width), dtype)

    base_table = _const_spec((tiles_per_seq, HEAD_DIM))
    local_table = _const_spec((STEP_ROWS, HEAD_DIM))
    memb =pl.BlockSpec((1, n_mem, M_WIDTH), lambda i: (i // tiles_per_seq, 0, 0))
    out_shape = ([res_shape(A_OUT, F32)] * 3 + [res_shape(A12_WIDTH, BF16)] * 3
                 + [jax.ShapeDtypeStruct((t, w), BF16) for w in (B_WIDTH, B_KV_WIDTH, B_KV_WIDTH, M_WIDTH)])
    out_specs = ([_res_spec(A_OUT)] * 3 + [_res_spec(A12_WIDTH)] * 3
                 + [row(w) for w in (B_WIDTH, B_KV_WIDTH, B_KV_WIDTH, M_WIDTH)])
    return pl.pallas_call(
        functools.partial(_proj_kernel, steps_per_seq=tiles_per_seq),
        out_shape=tuple(out_shape),
        grid=(t // STEP_ROWS,),
        in_specs=[row(D_MODEL), _const_spec((D_MODEL, D_IN)), _const_spec((ROW_TILE, ROW_TILE))]
        + [base_table] * 2 + [local_table] * 4 + [memb, memb],
        out_specs=tuple(out_specs),
        compiler_params=_params(),
        name="proj",
    )(u, w_in, perm, *rope_consts, mk, mv)


def _band_kernel(sink_ref, q_ref, k_ref, v_ref, kp_ref, vp_ref, o_ref, *, hq, hkv, max_dist, n_blk):
    grp = hq // hkv
    first_tile = pl.program_id(1) == 0

    row = lax.broadcasted_iota(jnp.int32, (BLOCK, 2 * BLOCK), 0)
    col = lax.broadcasted_iota(jnp.int32, (BLOCK, 2 * BLOCK), 1)
    dist = row + BLOCK - col
    band = (dist >= 0) & (dist <= max_dist)
    band_first = band & ((col >= BLOCK) | jnp.logical_not(first_tile))

    for j in range(n_blk):
        rows = slice(j * BLOCK, (j + 1) * BLOCK)
        mask = band_first if j == 0 else band
        for hk in range(hkv):
            kcols = slice(hk * HEAD_DIM, (hk + 1) * HEAD_DIM)
            if j == 0:
                kk = jnp.concatenate([kp_ref[0, :, kcols], k_ref[0, rows, kcols]], axis=0)
                vv = jnp.concatenate([vp_ref[0, :, kcols], v_ref[0, rows, kcols]], axis=0)
            else:
                both = slice((j - 1) * BLOCK, (j + 1) * BLOCK)
                kk = k_ref[0, both, kcols]
                vv = v_ref[0, both, kcols]
            for g in range(grp):
                h = hk * grp + g
                qcols = slice(h * HEAD_DIM, (h + 1) * HEAD_DIM)
                s = lax.dot_general(q_ref[0, rows, qcols], kk, NT_DIMS, preferred_element_type=F32)
                s = jnp.where(mask, s, NEG_INF)
                sk = sink_ref[h] * LOG2E
                m = jnp.maximum(jnp.max(s, axis=-1, keepdims=True), sk)
                p = jnp.exp2(s - m)
                tot = jnp.sum(p, axis=-1, keepdims=True) + jnp.exp2(sk - m)
                o = jnp.dot(p.astype(BF16), vv, preferred_element_type=F32)
                o_ref[0, rows, qcols] = (o / tot).astype(BF16)


def _band_attention(q, k, v, max_dist, sink):
    n, length, qw = q.shape
    kw = k.shape[2]
    rows = min(ATT_ROWS, length)
    n_blk = rows // BLOCK
    cur = lambda w: pl.BlockSpec((1, rows, w), lambda b, i: (b, i, 0))
    prev = pl.BlockSpec((1, BLOCK, kw), lambda b, i: (b, jnp.maximum(i * n_blk - 1, 0), 0))
    return pl.pallas_call(
        functools.partial(_band_kernel, hq=qw // HEAD_DIM, hkv=kw // HEAD_DIM, max_dist=max_dist,
                          n_blk=n_blk),
        out_shape=jax.ShapeDtypeStruct((n, length, qw), BF16),
        grid=(n, length // rows),
        in_specs=[pl.BlockSpec(memory_space=pltpu.SMEM), cur(qw), cur(kw), cur(kw), prev, prev],
        out_specs=cur(qw),
        compiler_params=_params(2),
        name="band_sink",
    )(sink, q, k, v, k, v)


def _dilated_kernel(q0_ref, k0_ref, v0_ref, k0p_ref, v0p_ref,
                    q12_ref, k12_ref, v12_ref, k1p_ref, v1p_ref, k2p_ref, v2p_ref,
                    o_ref, og_ref, lg_ref):
    first_chunk = pl.program_id(1) == 0
    row = lax.broadcasted_iota(jnp.int32, (BLOCK, 2 * BLOCK), 0)
    col = lax.broadcasted_iota(jnp.int32, (BLOCK, 2 * BLOCK), 1)
    own = col >= BLOCK
    kcol = jnp.where(own, col - BLOCK, col)

    def masks(n_pieces):
        piece = BLOCK // n_pieces
        shift = piece.bit_length() - 1
        pos = lambda r: (r & (piece - 1)) * n_pieces + (r >> shift)
        dist = pos(row) - pos(kcol) + jnp.where(own, 0, BLOCK)
        band = (dist >= 0) & (dist <= BLOCK)
        return band, band & (own | jnp.logical_not(first_chunk))

    def attend(q, kk, vv, mask):
        s = lax.dot_general(q, kk, NT_DIMS, preferred_element_type=F32)
        s = jnp.where(mask, s, NEG_INF)
        m = jnp.max(s, axis=-1, keepdims=True)
        p = jnp.exp2(s - m)
        den = jnp.sum(p, axis=-1, keepdims=True)
        o = jnp.dot(p.astype(BF16), vv, preferred_element_type=F32) / den
        return o, jnp.broadcast_to(m + jnp.log2(den), (BLOCK, HEAD_DIM))

    def gather(ref, starts, piece, cols):
        return jnp.concatenate([ref[s:s + piece, cols] for s in starts], axis=0).astype(BF16)

    for g, n_pieces in ((0, MAX_DIL), (1, MAX_DIL // DIL_PAIRS[1][1])):
        piece = BLOCK // n_pieces
        band, band_first = masks(n_pieces)
        q_ref, k_ref, v_ref, kp_ref, vp_ref = ((q0_ref, k0_ref, v0_ref, k0p_ref, v0p_ref) if g == 0 else
                                               (q12_ref, k12_ref, v12_ref, k1p_ref, v1p_ref))
        n_res = MAX_DIL // n_pieces
        for res in range(n_res):
            for h in range(A_HEADS_PER_GROUP):
                cols = slice(h * HEAD_DIM, (h + 1) * HEAD_DIM)
                bases = [(res + n_res * p) * BLOCK for p in range(n_pieces)]
                k_prev = jnp.concatenate([kp_ref[res + n_res * p, :, cols] for p in range(n_pieces)],
                                         axis=0).astype(BF16)
                v_prev = jnp.concatenate([vp_ref[res + n_res * p, :, cols] for p in range(n_pieces)],
                                         axis=0).astype(BF16)
                for a in range(BLOCK // piece):
                    starts = [b0 + a * piece for b0 in bases]
                    k_own = gather(k_ref, starts, piece, cols)
                    v_own = gather(v_ref, starts, piece, cols)
                    o, lse = attend(gather(q_ref, starts, piece, cols),
                                    jnp.concatenate([k_prev, k_own], axis=0),
                                    jnp.concatenate([v_prev, v_own], axis=0),
                                    band_first if a == 0 else band)
                    for p, s0 in enumerate(starts):
                        og_ref[g, s0:s0 + piece, cols] = o[p * piece:(p + 1) * piece]
                        lg_ref[g, s0:s0 + piece, cols] = lse[p * piece:(p + 1) * piece]
                    k_prev, v_prev = k_own, v_own

    band, band_first = masks(1)
    del band
    for c in range(MAX_DIL):
        rows = slice(c * BLOCK, (c + 1) * BLOCK)
        for h in range(A_HEADS_PER_GROUP):
            cols = slice(h * HEAD_DIM, (h + 1) * HEAD_DIM)
            cols2 = slice(A_OUT + h * HEAD_DIM, A_OUT + (h + 1) * HEAD_DIM)
            o2, l2 = attend(q12_ref[rows, cols2],
                            jnp.concatenate([k2p_ref[rows, cols], k12_ref[rows, cols2]], axis=0),
                            jnp.concatenate([v2p_ref[rows, cols], v12_ref[rows, cols2]], axis=0),
                            band_first)
            l0, l1 = lg_ref[0, rows, cols], lg_ref[1, rows, cols]
            mx = jnp.maximum(jnp.maximum(l0, l1), l2)
            e0, e1, e2 = jnp.exp2(l0 - mx), jnp.exp2(l1 - mx), jnp.exp2(l2 - mx)
            acc = e0 * og_ref[0, rows, cols] + e1 * og_ref[1, rows, cols] + e2 * o2
            o_ref[rows, cols] = (acc / (e0 + e1 + e2)).astype(BF16)


def _dilated_attention(q0, k0, v0, q12, k12, v12, batch):
    t = q0.shape[0]
    n_chunks = t // CHUNK
    per_seq = n_chunks // batch
    chunk = lambda b, i: b * per_seq + i
    prev = lambda b, i: b * per_seq + jnp.maximum(i - 1, 0)

    def cur(width):
        return pl.BlockSpec((CHUNK, width), lambda b, i: (chunk(b, i), 0))

    def tail(piece, width, col):
        return pl.BlockSpec((None, MAX_DIL, None, piece, width),
                            lambda b, i: (prev(b, i), 0, BLOCK // piece - 1, 0, col))

    def tail_view(x, piece):
        return x.reshape(n_chunks, MAX_DIL, BLOCK // piece, piece, x.shape[-1])

    piece0 = BLOCK // MAX_DIL
    piece1 = BLOCK // (MAX_DIL // DIL_PAIRS[1][1])
    prev2 = pl.BlockSpec((None, CHUNK, A_OUT), lambda b, i: (prev(b, i), 0, 1))
    view2 = lambda x: x.reshape(n_chunks, CHUNK, A12_WIDTH)
    return pl.pallas_call(
        _dilated_kernel,
        out_shape=jax.ShapeDtypeStruct((t, A_OUT), BF16),
        grid=(batch, per_seq),
        in_specs=[cur(A_OUT), cur(A_OUT), cur(A_OUT), tail(piece0, A_OUT, 0), tail(piece0, A_OUT, 0),
                  cur(A12_WIDTH), cur(A12_WIDTH), cur(A12_WIDTH),
                  tail(piece1, A_OUT, 0), tail(piece1, A_OUT, 0), prev2, prev2],
        out_specs=cur(A_OUT),
        scratch_shapes=[pltpu.VMEM((2, CHUNK, A_OUT), F32), pltpu.VMEM((2, CHUNK, A_OUT), F32)],
        compiler_params=_params(2),
        name="dilated",
    )(q0, k0, v0, tail_view(k0, piece0), tail_view(v0, piece0),
      q12, k12, v12, tail_view(k12, piece1), tail_view(v12, piece1), view2(k12), view2(v12))


def _merge_kernel(h_ref, u_ref, oa_ref, ob_ref, om_ref, unperm_ref,
                  wg_ref, bg_ref, woa_ref, wob_ref, wom_ref, wout_ref, gpost_ref, out_ref):
    for tile in range(MERGE_TILES):
        rows = slice(tile * MERGE_TILE, (tile + 1) * MERGE_TILE)
        u = u_ref[rows, :]

        def gate(idx):
            cols = slice(idx * D_MODEL, (idx + 1) * D_MODEL)
            z = jnp.dot(u, wg_ref[:, cols], preferred_element_type=F32) + bg_ref[:, cols]
            return 1.0 / (1.0 + jnp.exp(-z))

        oa_res = jnp.concatenate([oa_ref[c, tile] for c in range(MAX_DIL)], axis=0)
        o_a = jnp.dot(unperm_ref[...], oa_res, preferred_element_type=F32).astype(BF16)

        merged = gate(0) * jnp.dot(o_a, woa_ref[...], preferred_element_type=F32)
        merged = merged + gate(1) * jnp.dot(ob_ref[rows, :], wob_ref[...], preferred_element_type=F32)
        merged = merged + gate(2) * jnp.dot(om_ref[rows, :], wom_ref[...], preferred_element_type=F32)
        mixed = jnp.dot(merged.astype(BF16), wout_ref[...], preferred_element_type=F32)
        out_ref[rows, :] = h_ref[rows, :] + _rms(mixed, gpost_ref[...])


def _merge(h, u, oa, ob, om, unperm, wg, bg, woa, wob, wom, wout, gpost):
    t = h.shape[0]

    def row(width):
        return pl.BlockSpec((STEP_ROWS, width), lambda i: (i, 0))

    oa_spec = pl.BlockSpec((None, MAX_DIL, MERGE_TILES, MERGE_RUN, A_OUT),
                           lambda i: (i // STEPS_PER_CHUNK, 0, i % STEPS_PER_CHUNK, 0, 0))
    in_specs = ([row(D_MODEL), row(D_MODEL), oa_spec, row(B_WIDTH), row(M_WIDTH),
                 _const_spec((MERGE_TILE, MERGE_TILE)),
                 _const_spec((D_MODEL, 3 * D_MODEL)), _const_spec((1, 3 * D_MODEL)),
                 _const_spec((A_OUT, D_MODEL)), _const_spec((B_WIDTH, D_MODEL)),
                 _const_spec((M_WIDTH, D_MODEL)), _const_spec((D_MODEL, D_MODEL)),
                 _const_spec((1, D_MODEL))])
    oa = oa.reshape(t // CHUNK, MAX_DIL, CHUNK // MERGE_TILE, MERGE_RUN, A_OUT)
    return pl.pallas_call(
        _merge_kernel,
        out_shape=jax.ShapeDtypeStruct((t, D_MODEL), F32),
        grid=(t // STEP_ROWS,),
        in_specs=in_specs,
        out_specs=row(D_MODEL),
        compiler_params=_params(),
        name="merge",
    )(h, u, oa, ob, om, unperm, wg, bg, woa, wob, wom, wout, gpost)


def _rope_consts(seq):
    half = HEAD_DIM // 2
    inv = ROPE_THETA ** (-np.arange(half, dtype=np.float64) / half)
    inv = np.concatenate([inv, inv])[None, :]
    base = np.arange(0, seq, STEP_ROWS, dtype=np.float64)[:, None] * inv
    local = np.arange(STEP_ROWS, dtype=np.float64)[:, None] * inv
    local_res = (local.reshape(STEP_TILES, TILE_RUN, MAX_DIL, HEAD_DIM).transpose(0, 2, 1, 3)
                 .reshape(STEP_ROWS, HEAD_DIM))
    tables = (np.cos(base), np.sin(base), np.cos(local), np.sin(local), np.cos(local_res), np.sin(local_res))
    return tuple(jnp.asarray(tb, dtype=F32) for tb in tables)


def _residue_perm(tile):
    run = tile // MAX_DIL
    out_row = np.arange(tile)
    src = (out_row % run) * MAX_DIL + out_row // run
    return np.equal(src[:, None], np.arange(tile)[None, :]).astype(np.float32)


def kernel(x, mem, ffn1_norm_pre, ffn1_w_in, ffn1_w_out, ffn1_norm_post, mix_norm_pre, w_in, sinks, mem_norm, w_mem_kv, w_gate, b_gate, w_o_a, w_o_b, w_o_m, w_out, mix_norm_post, ffn2_norm_pre, ffn2_w_in, ffn2_w_out, ffn2_norm_post):
    b, s, _ = x.shape
    depth = ffn1_w_in.shape[0]
    t = b * s
    rope_consts = _rope_consts(s)
    perm = jnp.asarray(_residue_perm(ROW_TILE), dtype=BF16)
    unperm = jnp.asarray(_residue_perm(MERGE_TILE).T, dtype=BF16)
    h = x.reshape(t, D_MODEL)
    for l in range(depth):
        bf = lambda w: w[l].astype(BF16)
        mk, mv = _mem_kv(mem, mem_norm[l][None], bf(w_mem_kv))
        h1, u = _ffn(h, ffn1_norm_pre[l][None], bf(ffn1_w_in), bf(ffn1_w_out),
                     ffn1_norm_post[l][None], mix_norm_pre[l][None])
        q0, k0, v0, q12, k12, v12, bq, bk, bv, om = _proj(u, bf(w_in), perm, rope_consts, mk, mv, s)

        flat = lambda a: a.reshape(t, a.shape[-1])
        oa = _dilated_attention(flat(q0), flat(k0), flat(v0), flat(q12), flat(k12), flat(v12), b)
        seq3 = lambda a: a.reshape(b, s, a.shape[-1])
        ob = _band_attention(seq3(bq), seq3(bk), seq3(bv), B_WINDOW - 1, sinks[l]).reshape(t, B_WIDTH)

        h2 = _merge(h1, u, oa, ob, om, unperm, bf(w_gate), b_gate[l][None], bf(w_o_a), bf(w_o_b),
                    bf(w_o_m), bf(w_out), mix_norm_post[l][None])
        (h,) = _ffn(h2, ffn2_norm_pre[l][None], bf(ffn2_w_in), bf(ffn2_w_out), ffn2_norm_post[l][None])
    return h.reshape(b, s, D_MODEL)
```

```python
import functools

import jax
import jax.numpy as jnp
import numpy as np
from jax import lax
from jax.experimental import pallas as pl
from jax.experimental.pallas import tpu as pltpu

D_MODEL = 1024
HEAD_DIM = 128
DIL_PAIRS = ((128, 1), (512, 4), (2048, 16))
A_HEADS_PER_GROUP = 2
N_GROUPS = len(DIL_PAIRS)
A_HEADS = A_HEADS_PER_GROUP * N_GROUPS
B_Q_HEADS = 4
B_KV_HEADS = 2
B_WINDOW = 128
M_HEADS = 4
D_FF = 2816
ROPE_THETA = 10000.0
BLOCK = 128
EPS = 1e-6
NEG_INF = -1e30

A_WIDTH = A_HEADS * HEAD_DIM
A_OUT = A_HEADS_PER_GROUP * HEAD_DIM
B_WIDTH = B_Q_HEADS * HEAD_DIM
B_KV_WIDTH = B_KV_HEADS * HEAD_DIM
M_WIDTH = M_HEADS * HEAD_DIM
D_IN = 3 * A_WIDTH + B_WIDTH + 2 * B_KV_WIDTH + M_WIDTH
OFF_AQ = 0
OFF_AK = A_WIDTH
OFF_AV = 2 * A_WIDTH
OFF_BQ = 3 * A_WIDTH
OFF_BK = OFF_BQ + B_WIDTH
OFF_BV = OFF_BK + B_KV_WIDTH
OFF_MQ = OFF_BV + B_KV_WIDTH
QK_SCALE = HEAD_DIM ** -0.5
LOG2E = 1.4426950408889634
Q_SCALE = QK_SCALE * LOG2E

MXU_N = 256
ROW_TILE = 512
STEP_TILES = 2
STEP_ROWS = STEP_TILES * ROW_TILE
FFN_TILE = 256
FFN_TILES = STEP_ROWS // FFN_TILE
MERGE_TILE = 256
MERGE_TILES = STEP_ROWS // MERGE_TILE
ATT_ROWS = 4096
VMEM_LIMIT = 56 * 1024 * 1024

MAX_DIL = max(d for _, d in DIL_PAIRS)
CHUNK = MAX_DIL * BLOCK
TILES_PER_CHUNK = CHUNK // ROW_TILE
STEPS_PER_CHUNK = TILES_PER_CHUNK // STEP_TILES
TILE_RUN = ROW_TILE // MAX_DIL
MERGE_RUN = MERGE_TILE // MAX_DIL
A12_WIDTH = 2 * A_OUT

F32 = jnp.float32
BF16 = jnp.bfloat16
NT_DIMS = (((1,), (1,)), ((), ()))


def _rms(x, g):
    return x * lax.rsqrt(jnp.mean(x * x, axis=-1, keepdims=True) + EPS) * g


def _const_spec(shape):
    nd = len(shape)
    return pl.BlockSpec(shape, lambda *_: (0,) * nd, pipeline_mode=pl.Buffered(1))


def _params(n_axes=1):
    return pltpu.CompilerParams(
        dimension_semantics=("arbitrary",) * n_axes, vmem_limit_bytes=VMEM_LIMIT)


def _mem_kv_kernel(mem_ref, g_ref, w_ref, mk_ref, mv_ref):
    mn = _rms(mem_ref[0], g_ref[...]).astype(BF16)
    kv = jnp.dot(mn, w_ref[...], preferred_element_type=F32)
    mk_ref[0] = kv[:, :M_WIDTH].astype(BF16)
    mv_ref[0] = kv[:, M_WIDTH:].astype(BF16)


def _mem_kv(mem, g, w):
    b, n, _ = mem.shape
    out = jax.ShapeDtypeStruct((b, n, M_WIDTH), BF16)
    return pl.pallas_call(
        _mem_kv_kernel,
        out_shape=(out, out),
        grid=(b,),
        in_specs=[pl.BlockSpec((1, n, D_MODEL), lambda i: (i, 0, 0)),
                  _const_spec((1, D_MODEL)),
                  _const_spec((D_MODEL, 2 * M_WIDTH))],
        out_specs=(pl.BlockSpec((1, n, M_WIDTH), lambda i: (i, 0, 0)),
                   pl.BlockSpec((1, n, M_WIDTH), lambda i: (i, 0, 0))),
        compiler_params=_params(),
        name="mem_kv",
    )(mem, g, w)


def _ffn_kernel(x_ref, gpre_ref, win_ref, wout_ref, gpost_ref, *rest, emit_u):
    if emit_u:
        gnext_ref, h_ref, u_ref, xn_ref, act_ref, f_ref = rest
    else:
        h_ref, xn_ref, act_ref, f_ref = rest
    rows = lambda tile: slice(tile * FFN_TILE, (tile + 1) * FFN_TILE)

    def prologue(tile):
        xn_ref[tile] = _rms(x_ref[rows(tile), :], gpre_ref[...]).astype(BF16)

    def up_chunk(tile, c):
        lo = c * MXU_N
        xn = xn_ref[tile]
        gate = jnp.dot(xn, win_ref[:, lo:lo + MXU_N], preferred_element_type=F32)
        up = jnp.dot(xn, win_ref[:, D_FF + lo:D_FF + lo + MXU_N], preferred_element_type=F32)
        silu = gate * (1.0 / (1.0 + jnp.exp(-gate)))
        act_ref[tile, :, lo:lo + MXU_N] = (silu * up).astype(BF16)

    def epilogue(tile):
        h = x_ref[rows(tile), :] + 0.5 * _rms(f_ref[tile], gpost_ref[...])
        h_ref[rows(tile), :] = h
        if emit_u:
            u_ref[rows(tile), :] = _rms(h, gnext_ref[...]).astype(BF16)

    n_chunks = D_FF // MXU_N
    prologue(0)
    for c in range(n_chunks):
        up_chunk(0, c)
    for tile in range(FFN_TILES):
        if tile + 1 < FFN_TILES:
            prologue(tile + 1)
            up_chunk(tile + 1, 0)
        f_ref[tile] = jnp.dot(act_ref[tile], wout_ref[...], preferred_element_type=F32)
        epilogue(tile)
        if tile + 1 < FFN_TILES:
            for c in range(1, n_chunks):
                up_chunk(tile + 1, c)


def _ffn(x, gpre, w_in, w_out, gpost, gnext=None):
    t = x.shape[0]
    emit_u = gnext is not None
    row = pl.BlockSpec((STEP_ROWS, D_MODEL), lambda i: (i, 0))
    gain = _const_spec((1, D_MODEL))
    in_specs = [row, gain, _const_spec((D_MODEL, 2 * D_FF)), _const_spec((D_FF, D_MODEL)), gain]
    args = [x, gpre, w_in, w_out, gpost]
    out_shape = [jax.ShapeDtypeStruct((t, D_MODEL), F32)]
    out_specs = [row]
    if emit_u:
        in_specs.append(gain)
        args.append(gnext)
        out_shape.append(jax.ShapeDtypeStruct((t, D_MODEL), BF16))
        out_specs.append(row)
    return pl.pallas_call(
        functools.partial(_ffn_kernel, emit_u=emit_u),
        out_shape=tuple(out_shape),
        grid=(t // STEP_ROWS,),
        in_specs=in_specs,
        out_specs=tuple(out_specs),
        scratch_shapes=[pltpu.VMEM((FFN_TILES, FFN_TILE, D_MODEL), BF16),
                        pltpu.VMEM((FFN_TILES, FFN_TILE, D_FF), BF16),
                        pltpu.VMEM((FFN_TILES, FFN_TILE, D_MODEL), F32)],
        compiler_params=_params(),
        name="ffn_u" if emit_u else "ffn",
    )(*args)


def _proj_kernel(u_ref, w_ref, perm_ref, cb_ref, sb_ref, cl_ref, sl_ref, clr_ref, slr_ref, mk_ref, mv_ref,
                 q0_ref, k0_ref, v0_ref, q12_ref, k12_ref, v12_ref, bq_ref, bk_ref, bv_ref, om_ref,
                 *, steps_per_seq):
    step = pl.program_id(0) % steps_per_seq
    cb, sb = cb_ref[pl.ds(step, 1), :], sb_ref[pl.ds(step, 1), :]
    lane = lax.broadcasted_iota(jnp.int32, (1, HEAD_DIM), 1)
    sign = jnp.where(lane < HEAD_DIM // 2, -1.0, 1.0)

    def rope_tables(cl, sl, scale):
        cos = (cb * scale) * cl - (sb * scale) * sl
        sin = (sb * (sign * scale)) * cl + (cb * (sign * scale)) * sl
        return cos, sin

    for tile in range(STEP_TILES):
        rows = slice(tile * ROW_TILE, (tile + 1) * ROW_TILE)
        cl, sl, clr, slr = cl_ref[rows, :], sl_ref[rows, :], clr_ref[rows, :], slr_ref[rows, :]
        _proj_tile(tile, u_ref, w_ref, perm_ref,
                   rope_tables(cl, sl, Q_SCALE) + rope_tables(cl, sl, 1.0),
                   rope_tables(clr, slr, Q_SCALE) + rope_tables(clr, slr, 1.0), mk_ref, mv_ref,
                   q0_ref, k0_ref, v0_ref, q12_ref, k12_ref, v12_ref, bq_ref, bk_ref, bv_ref, om_ref)


def _proj_tile(tile, u_ref, w_ref, perm_ref, tables, tables_res, mk_ref, mv_ref,
               q0_ref, k0_ref, v0_ref, q12_ref, k12_ref, v12_ref, bq_ref, bk_ref, bv_ref, om_ref):
    rows = slice(tile * ROW_TILE, (tile + 1) * ROW_TILE)
    u = u_ref[rows, :]
    u_res = jnp.dot(perm_ref[...], u, preferred_element_type=F32).astype(BF16)

    def proj(lhs, lo):
        return jnp.dot(lhs, w_ref[:, lo:lo + MXU_N], preferred_element_type=F32)

    def rope(y, cos, sin):
        out = []
        for hh in range(MXU_N // HEAD_DIM):
            yh = y[:, hh * HEAD_DIM:(hh + 1) * HEAD_DIM]
            out.append(yh * cos + pltpu.roll(yh, HEAD_DIM // 2, 1) * sin)
        return jnp.concatenate(out, axis=1)

    def put_res(dst, col, y):
        for c in range(MAX_DIL):
            dst[c, tile, :, col:col + MXU_N] = y[c * TILE_RUN:(c + 1) * TILE_RUN].astype(dst.dtype)

    cqp, sqp, ckp, skp = tables_res
    for g in range(N_GROUPS):
        dst_q, dst_k, dst_v = (q0_ref, k0_ref, v0_ref) if g == 0 else (q12_ref, k12_ref, v12_ref)
        col = 0 if g == 0 else (g - 1) * A_OUT
        put_res(dst_q, col, rope(proj(u_res, OFF_AQ + g * A_OUT), cqp, sqp))
        put_res(dst_k, col, rope(proj(u_res, OFF_AK + g * A_OUT), ckp, skp))
        put_res(dst_v, col, proj(u_res, OFF_AV + g * A_OUT))

    cq, sq, ck, sk = tables
    for c in range(B_WIDTH // MXU_N):
        bq_ref[rows, c * MXU_N:(c + 1) * MXU_N] = rope(proj(u, OFF_BQ + c * MXU_N), cq, sq).astype(BF16)
    bk_ref[rows, :] = rope(proj(u, OFF_BK), ck, sk).astype(BF16)
    bv_ref[rows, :] = proj(u, OFF_BV).astype(BF16)

    for c in range(M_WIDTH // MXU_N):
        y = proj(u, OFF_MQ + c * MXU_N) * Q_SCALE
        for hh in range(MXU_N // HEAD_DIM):
            h = c * (MXU_N // HEAD_DIM) + hh
            cols = slice(h * HEAD_DIM, (h + 1) * HEAD_DIM)
            q = y[:, hh * HEAD_DIM:(hh + 1) * HEAD_DIM].astype(BF16)
            s = lax.dot_general(q, mk_ref[0, :, cols], NT_DIMS, preferred_element_type=F32)
            m = jnp.max(s, axis=-1, keepdims=True)
            p = jnp.exp2(s - m)
            den = jnp.sum(p, axis=-1, keepdims=True)
            o = jnp.dot(p.astype(BF16), mv_ref[0, :, cols], preferred_element_type=F32)
            om_ref[rows, cols] = (o / den).astype(BF16)


def _res_spec(width):
    return pl.BlockSpec((None, MAX_DIL, STEP_TILES, TILE_RUN, width),
                        lambda i: (i // STEPS_PER_CHUNK, 0, i % STEPS_PER_CHUNK, 0, 0))


def _proj(u, w_in, perm, rope_consts, mk, mv, seq):
    t = u.shape[0]
    n_mem = mk.shape[1]
    tiles_per_seq = seq // STEP_ROWS

    def row(width):
        return pl.BlockSpec((STEP_ROWS, width), lambda i: (i, 0))

    def res_shape(width, dtype):
        return jax.ShapeDtypeStruct((t // CHUNK, MAX_DIL, TILES_PER_CHUNK, TILE_RUN, width), dtype)

    base_table = _const_spec((tiles_per_seq, HEAD_DIM))
    local_table = _const_spec((STEP_ROWS, HEAD_DIM))
    memb =pl.BlockSpec((1, n_mem, M_WIDTH), lambda i: (i // tiles_per_seq, 0, 0))
    out_shape = ([res_shape(A_OUT, F32)] * 3 + [res_shape(A12_WIDTH, BF16)] * 3
                 + [jax.ShapeDtypeStruct((t, w), BF16) for w in (B_WIDTH, B_KV_WIDTH, B_KV_WIDTH, M_WIDTH)])
    out_specs = ([_res_spec(A_OUT)] * 3 + [_res_spec(A12_WIDTH)] * 3
                 + [row(w) for w in (B_WIDTH, B_KV_WIDTH, B_KV_WIDTH, M_WIDTH)])
    return pl.pallas_call(
        functools.partial(_proj_kernel, steps_per_seq=tiles_per_seq),
        out_shape=tuple(out_shape),
        grid=(t // STEP_ROWS,),
        in_specs=[row(D_MODEL), _const_spec((D_MODEL, D_IN)), _const_spec((ROW_TILE, ROW_TILE))]
        + [base_table] * 2 + [local_table] * 4 + [memb, memb],
        out_specs=tuple(out_specs),
        compiler_params=_params(),
        name="proj",
    )(u, w_in, perm, *rope_consts, mk, mv)


def _band_kernel(sink_ref, q_ref, k_ref, v_ref, kp_ref, vp_ref, o_ref, *, hq, hkv, max_dist, n_blk):
    grp = hq // hkv
    first_tile = pl.program_id(1) == 0

    row = lax.broadcasted_iota(jnp.int32, (BLOCK, 2 * BLOCK), 0)
    col = lax.broadcasted_iota(jnp.int32, (BLOCK, 2 * BLOCK), 1)
    dist = row + BLOCK - col
    band = (dist >= 0) & (dist <= max_dist)
    band_first = band & ((col >= BLOCK) | jnp.logical_not(first_tile))

    for j in range(n_blk):
        rows = slice(j * BLOCK, (j + 1) * BLOCK)
        mask = band_first if j == 0 else band
        for hk in range(hkv):
            kcols = slice(hk * HEAD_DIM, (hk + 1) * HEAD_DIM)
            if j == 0:
                kk = jnp.concatenate([kp_ref[0, :, kcols], k_ref[0, rows, kcols]], axis=0)
                vv = jnp.concatenate([vp_ref[0, :, kcols], v_ref[0, rows, kcols]], axis=0)
            else:
                both = slice((j - 1) * BLOCK, (j + 1) * BLOCK)
                kk = k_ref[0, both, kcols]
                vv = v_ref[0, both, kcols]
            for g in range(grp):
                h = hk * grp + g
                qcols = slice(h * HEAD_DIM, (h + 1) * HEAD_DIM)
                s = lax.dot_general(q_ref[0, rows, qcols], kk, NT_DIMS, preferred_element_type=F32)
                s = jnp.where(mask, s, NEG_INF)
                sk = sink_ref[h] * LOG2E
                m = jnp.maximum(jnp.max(s, axis=-1, keepdims=True), sk)
                p = jnp.exp2(s - m)
                tot = jnp.sum(p, axis=-1, keepdims=True) + jnp.exp2(sk - m)
                o = jnp.dot(p.astype(BF16), vv, preferred_element_type=F32)
                o_ref[0, rows, qcols] = (o / tot).astype(BF16)


def _band_attention(q, k, v, max_dist, sink):
    n, length, qw = q.shape
    kw = k.shape[2]
    rows = min(ATT_ROWS, length)
    n_blk = rows // BLOCK
    cur = lambda w: pl.BlockSpec((1, rows, w), lambda b, i: (b, i, 0))
    prev = pl.BlockSpec((1, BLOCK, kw), lambda b, i: (b, jnp.maximum(i * n_blk - 1, 0), 0))
    return pl.pallas_call(
        functools.partial(_band_kernel, hq=qw // HEAD_DIM, hkv=kw // HEAD_DIM, max_dist=max_dist,
                          n_blk=n_blk),
        out_shape=jax.ShapeDtypeStruct((n, length, qw), BF16),
        grid=(n, length // rows),
        in_specs=[pl.BlockSpec(memory_space=pltpu.SMEM), cur(qw), cur(kw), cur(kw), prev, prev],
        out_specs=cur(qw),
        compiler_params=_params(2),
        name="band_sink",
    )(sink, q, k, v, k, v)


def _dilated_kernel(q0_ref, k0_ref, v0_ref, k0p_ref, v0p_ref,
                    q12_ref, k12_ref, v12_ref, k1p_ref, v1p_ref, k2p_ref, v2p_ref,
                    o_ref, og_ref, lg_ref):
    first_chunk = pl.program_id(1) == 0
    row = lax.broadcasted_iota(jnp.int32, (BLOCK, 2 * BLOCK), 0)
    col = lax.broadcasted_iota(jnp.int32, (BLOCK, 2 * BLOCK), 1)
    own = col >= BLOCK
    kcol = jnp.where(own, col - BLOCK, col)

    def masks(n_pieces):
        piece = BLOCK // n_pieces
        shift = piece.bit_length() - 1
        pos = lambda r: (r & (piece - 1)) * n_pieces + (r >> shift)
        dist = pos(row) - pos(kcol) + jnp.where(own, 0, BLOCK)
        band = (dist >= 0) & (dist <= BLOCK)
        return band, band & (own | jnp.logical_not(first_chunk))

    def attend(q, kk, vv, mask):
        s = lax.dot_general(q, kk, NT_DIMS, preferred_element_type=F32)
        s = jnp.where(mask, s, NEG_INF)
        m = jnp.max(s, axis=-1, keepdims=True)
        p = jnp.exp2(s - m)
        den = jnp.sum(p, axis=-1, keepdims=True)
        o = jnp.dot(p.astype(BF16), vv, preferred_element_type=F32) / den
        return o, jnp.broadcast_to(m + jnp.log2(den), (BLOCK, HEAD_DIM))

    def gather(ref, starts, piece, cols):
        return jnp.concatenate([ref[s:s + piece, cols] for s in starts], axis=0).astype(BF16)

    for g, n_pieces in ((0, MAX_DIL), (1, MAX_DIL // DIL_PAIRS[1][1])):
        piece = BLOCK // n_pieces
        band, band_first = masks(n_pieces)
        q_ref, k_ref, v_ref, kp_ref, vp_ref = ((q0_ref, k0_ref, v0_ref, k0p_ref, v0p_ref) if g == 0 else
                                               (q12_ref, k12_ref, v12_ref, k1p_ref, v1p_ref))
        n_res = MAX_DIL // n_pieces
        for res in range(n_res):
            for h in range(A_HEADS_PER_GROUP):
                cols = slice(h * HEAD_DIM, (h + 1) * HEAD_DIM)
                bases = [(res + n_res * p) * BLOCK for p in range(n_pieces)]
                k_prev = jnp.concatenate([kp_ref[res + n_res * p, :, cols] for p in range(n_pieces)],
                                         axis=0).astype(BF16)
                v_prev = jnp.concatenate([vp_ref[res + n_res * p, :, cols] for p in range(n_pieces)],
                                         axis=0).astype(BF16)
                for a in range(BLOCK // piece):
                    starts = [b0 + a * piece for b0 in bases]
                    k_own = gather(k_ref, starts, piece, cols)
                    v_own = gather(v_ref, starts, piece, cols)
                    o, lse = attend(gather(q_ref, starts, piece, cols),
                                    jnp.concatenate([k_prev, k_own], axis=0),
                                    jnp.concatenate([v_prev, v_own], axis=0),
                                    band_first if a == 0 else band)
                    for p, s0 in enumerate(starts):
                        og_ref[g, s0:s0 + piece, cols] = o[p * piece:(p + 1) * piece]
                        lg_ref[g, s0:s0 + piece, cols] = lse[p * piece:(p + 1) * piece]
                    k_prev, v_prev = k_own, v_own

    band, band_first = masks(1)
    del band
    for c in range(MAX_DIL):
        rows = slice(c * BLOCK, (c + 1) * BLOCK)
        for h in range(A_HEADS_PER_GROUP):
            cols = slice(h * HEAD_DIM, (h + 1) * HEAD_DIM)
            cols2 = slice(A_OUT + h * HEAD_DIM, A_OUT + (h + 1) * HEAD_DIM)
            o2, l2 = attend(q12_ref[rows, cols2],
                            jnp.concatenate([k2p_ref[rows, cols], k12_ref[rows, cols2]], axis=0),
                            jnp.concatenate([v2p_ref[rows, cols], v12_ref[rows, cols2]], axis=0),
                            band_first)
            l0, l1 = lg_ref[0, rows, cols], lg_ref[1, rows, cols]
            mx = jnp.maximum(jnp.maximum(l0, l1), l2)
            e0, e1, e2 = jnp.exp2(l0 - mx), jnp.exp2(l1 - mx), jnp.exp2(l2 - mx)
            acc = e0 * og_ref[0, rows, cols] + e1 * og_ref[1, rows, cols] + e2 * o2
            o_ref[rows, cols] = (acc / (e0 + e1 + e2)).astype(BF16)


def _dilated_attention(q0, k0, v0, q12, k12, v12, batch):
    t = q0.shape[0]
    n_chunks = t // CHUNK
    per_seq = n_chunks // batch
    chunk = lambda b, i: b * per_seq + i
    prev = lambda b, i: b * per_seq + jnp.maximum(i - 1, 0)

    def cur(width):
        return pl.BlockSpec((CHUNK, width), lambda b, i: (chunk(b, i), 0))

    def tail(piece, width, col):
        return pl.BlockSpec((None, MAX_DIL, None, piece, width),
                            lambda b, i: (prev(b, i), 0, BLOCK // piece - 1, 0, col))

    def tail_view(x, piece):
        return x.reshape(n_chunks, MAX_DIL, BLOCK // piece, piece, x.shape[-1])

    piece0 = BLOCK // MAX_DIL
    piece1 = BLOCK // (MAX_DIL // DIL_PAIRS[1][1])
    prev2 = pl.BlockSpec((None, CHUNK, A_OUT), lambda b, i: (prev(b, i), 0, 1))
    view2 = lambda x: x.reshape(n_chunks, CHUNK, A12_WIDTH)
    return pl.pallas_call(
        _dilated_kernel,
        out_shape=jax.ShapeDtypeStruct((t, A_OUT), BF16),
        grid=(batch, per_seq),
        in_specs=[cur(A_OUT), cur(A_OUT), cur(A_OUT), tail(piece0, A_OUT, 0), tail(piece0, A_OUT, 0),
                  cur(A12_WIDTH), cur(A12_WIDTH), cur(A12_WIDTH),
                  tail(piece1, A_OUT, 0), tail(piece1, A_OUT, 0), prev2, prev2],
        out_specs=cur(A_OUT),
        scratch_shapes=[pltpu.VMEM((2, CHUNK, A_OUT), F32), pltpu.VMEM((2, CHUNK, A_OUT), F32)],
        compiler_params=_params(2),
        name="dilated",
    )(q0, k0, v0, tail_view(k0, piece0), tail_view(v0, piece0),
      q12, k12, v12, tail_view(k12, piece1), tail_view(v12, piece1), view2(k12), view2(v12))


def _merge_kernel(h_ref, u_ref, oa_ref, ob_ref, om_ref, unperm_ref,
                  wg_ref, bg_ref, woa_ref, wob_ref, wom_ref, wout_ref, gpost_ref, out_ref):
    for tile in range(MERGE_TILES):
        rows = slice(tile * MERGE_TILE, (tile + 1) * MERGE_TILE)
        u = u_ref[rows, :]

        def gate(idx):
            cols = slice(idx * D_MODEL, (idx + 1) * D_MODEL)
            z = jnp.dot(u, wg_ref[:, cols], preferred_element_type=F32) + bg_ref[:, cols]
            return 1.0 / (1.0 + jnp.exp(-z))

        oa_res = jnp.concatenate([oa_ref[c, tile] for c in range(MAX_DIL)], axis=0)
        o_a = jnp.dot(unperm_ref[...], oa_res, preferred_element_type=F32).astype(BF16)

        merged = gate(0) * jnp.dot(o_a, woa_ref[...], preferred_element_type=F32)
        merged = merged + gate(1) * jnp.dot(ob_ref[rows, :], wob_ref[...], preferred_element_type=F32)
        merged = merged + gate(2) * jnp.dot(om_ref[rows, :], wom_ref[...], preferred_element_type=F32)
        mixed = jnp.dot(merged.astype(BF16), wout_ref[...], preferred_element_type=F32)
        out_ref[rows, :] = h_ref[rows, :] + _rms(mixed, gpost_ref[...])


def _merge(h, u, oa, ob, om, unperm, wg, bg, woa, wob, wom, wout, gpost):
    t = h.shape[0]

    def row(width):
        return pl.BlockSpec((STEP_ROWS, width), lambda i: (i, 0))

    oa_spec = pl.BlockSpec((None, MAX_DIL, MERGE_TILES, MERGE_RUN, A_OUT),
                           lambda i: (i // STEPS_PER_CHUNK, 0, i % STEPS_PER_CHUNK, 0, 0))
    in_specs = ([row(D_MODEL), row(D_MODEL), oa_spec, row(B_WIDTH), row(M_WIDTH),
                 _const_spec((MERGE_TILE, MERGE_TILE)),
                 _const_spec((D_MODEL, 3 * D_MODEL)), _const_spec((1, 3 * D_MODEL)),
                 _const_spec((A_OUT, D_MODEL)), _const_spec((B_WIDTH, D_MODEL)),
                 _const_spec((M_WIDTH, D_MODEL)), _const_spec((D_MODEL, D_MODEL)),
                 _const_spec((1, D_MODEL))])
    oa = oa.reshape(t // CHUNK, MAX_DIL, CHUNK // MERGE_TILE, MERGE_RUN, A_OUT)
    return pl.pallas_call(
        _merge_kernel,
        out_shape=jax.ShapeDtypeStruct((t, D_MODEL), F32),
        grid=(t // STEP_ROWS,),
        in_specs=in_specs,
        out_specs=row(D_MODEL),
        compiler_params=_params(),
        name="merge",
    )(h, u, oa, ob, om, unperm, wg, bg, woa, wob, wom, wout, gpost)


def _rope_consts(seq):
    half = HEAD_DIM // 2
    inv = ROPE_THETA ** (-np.arange(half, dtype=np.float64) / half)
    inv = np.concatenate([inv, inv])[None, :]
    base = np.arange(0, seq, STEP_ROWS, dtype=np.float64)[:, None] * inv
    local = np.arange(STEP_ROWS, dtype=np.float64)[:, None] * inv
    local_res = (local.reshape(STEP_TILES, TILE_RUN, MAX_DIL, HEAD_DIM).transpose(0, 2, 1, 3)
                 .reshape(STEP_ROWS, HEAD_DIM))
    tables = (np.cos(base), np.sin(base), np.cos(local), np.sin(local), np.cos(local_res), np.sin(local_res))
    return tuple(jnp.asarray(tb, dtype=F32) for tb in tables)


def _residue_perm(tile):
    run = tile // MAX_DIL
    out_row = np.arange(tile)
    src = (out_row % run) * MAX_DIL + out_row // run
    return np.equal(src[:, None], np.arange(tile)[None, :]).astype(np.float32)


def kernel(x, mem, ffn1_norm_pre, ffn1_w_in, ffn1_w_out, ffn1_norm_post, mix_norm_pre, w_in, sinks, mem_norm, w_mem_kv, w_gate, b_gate, w_o_a, w_o_b, w_o_m, w_out, mix_norm_post, ffn2_norm_pre, ffn2_w_in, ffn2_w_out, ffn2_norm_post):
    b, s, _ = x.shape
    depth = ffn1_w_in.shape[0]
    t = b * s
    rope_consts = _rope_consts(s)
    perm = jnp.asarray(_residue_perm(ROW_TILE), dtype=BF16)
    unperm = jnp.asarray(_residue_perm(MERGE_TILE).T, dtype=BF16)
    h = x.reshape(t, D_MODEL)
    for l in range(depth):
        bf = lambda w: w[l].astype(BF16)
        mk, mv = _mem_kv(mem, mem_norm[l][None], bf(w_mem_kv))
        h1, u = _ffn(h, ffn1_norm_pre[l][None], bf(ffn1_w_in), bf(ffn1_w_out),
                     ffn1_norm_post[l][None], mix_norm_pre[l][None])
        q0, k0, v0, q12, k12, v12, bq, bk, bv, om = _proj(u, bf(w_in), perm, rope_consts, mk, mv, s)

        flat = lambda a: a.reshape(t, a.shape[-1])
        oa = _dilated_attention(flat(q0), flat(k0), flat(v0), flat(q12), flat(k12), flat(v12), b)
        seq3 = lambda a: a.reshape(b, s, a.shape[-1])
        ob = _band_attention(seq3(bq), seq3(bk), seq3(bv), B_WINDOW - 1, sinks[l]).reshape(t, B_WIDTH)

        h2 = _merge(h1, u, oa, ob, om, unperm, bf(w_gate), b_gate[l][None], bf(w_o_a), bf(w_o_b),
                    bf(w_o_m), bf(w_out), mix_norm_post[l][None])
        (h,) = _ffn(h2, ffn2_norm_pre[l][None], bf(ffn2_w_in), bf(ffn2_w_out), ffn2_norm_post[l][None])
    return h.reshape(b, s, D_MODEL)
```
